```python
import math
import jax, jax.numpy as jnp
from jax import lax
import numpy as np

D_MODEL = 2048
BATCH = 4
SEQ = 4096
DEPTH = 1

N_META = 16
ROPE_THETA = 10000.0
NORM_EPS = 1e-6
DA_HEADS = 4
DA_HEAD_DIM = 128
DA_V_DIM = 2 * DA_HEAD_DIM
DA_Q_WIDTH = DA_HEADS * 2 * DA_HEAD_DIM
DA_V_WIDTH = DA_HEADS * DA_V_DIM
Q_BLOCK = 128
RET_HEADS = 4
RET_QK_DIM = 256
RET_V_DIM = 256
RET_QK_WIDTH = RET_HEADS * RET_QK_DIM
RET_V_WIDTH = RET_HEADS * RET_V_DIM
CHUNK = 128
D_FF = 4 * D_MODEL
IN_WIDTH = 2 * DA_Q_WIDTH + DA_V_WIDTH + 2 * RET_QK_WIDTH + 2 * RET_V_WIDTH + 2 * D_MODEL

kernel_name = 'hybrid_diffattn_retention_gated_block'


def _rms(x):
    xf = x.astype(jnp.float32)
    return xf * lax.rsqrt(jnp.mean(xf * xf, axis=-1, keepdims=True) + NORM_EPS)


def rmsnorm(x, g):
    return (_rms(x) * g.astype(jnp.float32)).astype(x.dtype)


def rope(x, pos):
    d = x.shape[-1]
    half = d // 2
    inv = ROPE_THETA ** (-jnp.arange(half, dtype=jnp.float32) / half)
    ang = pos.astype(jnp.float32)[:, None] * inv[None, :]
    cos = jnp.cos(ang).astype(x.dtype)
    sin = jnp.sin(ang).astype(x.dtype)
    x1, x2 = x[..., :half], x[..., half:]
    return jnp.concatenate([x1 * cos - x2 * sin, x2 * cos + x1 * sin], axis=-1)


def diff_attention(q, k, v, lam, lambda_init, subln_g):
    B, H, _, L, dh = q.shape
    nb = -(-L // Q_BLOCK)
    Lq = nb * Q_BLOCK
    qp = jnp.pad(q, ((0, 0), (0, 0), (0, 0), (0, Lq - L), (0, 0)))
    qb = qp.reshape(B, H, 2, nb, Q_BLOCK, dh).transpose(3, 0, 1, 2, 4, 5)
    scale = dh ** -0.5
    kpos = jnp.arange(L)

    def block(args):
        qblk, i = args
        s = jnp.einsum('bhcqd,bhckd->bhcqk', qblk, k).astype(jnp.float32) * scale
        qpos = i * Q_BLOCK + jnp.arange(Q_BLOCK)
        mask = kpos[None, :] <= qpos[:, None]
        p = jax.nn.softmax(jnp.where(mask, s, -jnp.inf), axis=-1)
        a = p[:, :, 0] - lam * p[:, :, 1]
        return jnp.einsum('bhqk,bhkd->bhqd', a.astype(v.dtype), v)

    o = lax.map(block, (qb, jnp.arange(nb)))
    o = o.transpose(1, 2, 0, 3, 4).reshape(B, H, Lq, v.shape[-1])[:, :, :L]
    return rmsnorm(o, subln_g) * (1.0 - lambda_init)


def retention(q, k, v, log_gamma):
    in_dtype = v.dtype
    q, k, v = (t.astype(jnp.float32) for t in (q, k, v))
    B, H, L, dk = q.shape
    dv = v.shape[-1]
    pad = (-L) % CHUNK
    padw = ((0, 0), (0, 0), (pad, 0), (0, 0))
    n = (L + pad) // CHUNK

    def chunks(t):
        t = jnp.pad(t, padw)
        return t.reshape(B, H, n, CHUNK, t.shape[-1]).transpose(2, 0, 1, 3, 4)

    qc, kc, vc = chunks(q), chunks(k), chunks(v)
    idx = jnp.arange(CHUNK, dtype=jnp.float32)
    lg = log_gamma[:, None, None]
    rel = idx[:, None] - idx[None, :]
    decay_mask = jnp.where(rel >= 0, jnp.exp(lg * jnp.maximum(rel, 0.0)), 0.0)
    xi = jnp.exp(log_gamma[:, None] * (idx[None, :] + 1.0))
    zeta = jnp.exp(log_gamma[:, None] * (CHUNK - 1.0 - idx[None, :]))
    chunk_decay = jnp.exp(log_gamma * CHUNK)

    def step(R, inp):
        qi, ki, vi = inp
        s = jnp.einsum('bhqd,bhkd->bhqk', qi, ki) * decay_mask
        inner = jnp.einsum('bhqk,bhkd->bhqd', s, vi)
        cross = jnp.einsum('bhqd,bhde->bhqe', qi, R) * xi[None, :, :, None]
        R = chunk_decay[None, :, None, None] * R + jnp.einsum(
            'bhkd,bhke->bhde', ki, vi * zeta[None, :, :, None])
        return R, inner + cross

    R0 = jnp.zeros((B, H, dk, dv), jnp.float32)
    _, o = lax.scan(step, R0, (qc, kc, vc))
    o = o.transpose(1, 2, 0, 3, 4).reshape(B, H, n * CHUNK, dv)[:, :, pad:]
    return o.astype(in_dtype)


def hybrid_layer(h, pos, lambda_init, norm1_g, w_in, lam_q1, lam_k1, lam_q2, lam_k2,
                 da_subln_g, w_pa, w_pr, w_o, norm2_g, w_up, w_down):
    B, L, D = h.shape
    xn = rmsnorm(h, norm1_g)
    proj = xn @ w_in
    sizes = [DA_Q_WIDTH, DA_Q_WIDTH, DA_V_WIDTH, RET_QK_WIDTH, RET_QK_WIDTH,
             RET_V_WIDTH, RET_V_WIDTH, D_MODEL, D_MODEL]
    cuts = [sum(sizes[:i + 1]) for i in range(len(sizes) - 1)]
    da_q, da_k, da_v, r_q, r_k, r_v, r_g, g_a, g_r = jnp.split(proj, cuts, axis=-1)

    q = da_q.reshape(B, L, DA_HEADS, 2, DA_HEAD_DIM).transpose(0, 2, 3, 1, 4)
    k = da_k.reshape(B, L, DA_HEADS, 2, DA_HEAD_DIM).transpose(0, 2, 3, 1, 4)
    v = da_v.reshape(B, L, DA_HEADS, DA_V_DIM).transpose(0, 2, 1, 3)
    q, k = rope(q, pos), rope(k, pos)
    lam = (jnp.exp(jnp.sum(lam_q1.astype(jnp.float32) * lam_k1.astype(jnp.float32)))
           - jnp.exp(jnp.sum(lam_q2.astype(jnp.float32) * lam_k2.astype(jnp.float32)))
           + lambda_init)
    oa = diff_attention(q, k, v, lam, lambda_init, da_subln_g)
    y_a = oa.transpose(0, 2, 1, 3).reshape(B, L, DA_V_WIDTH) @ w_pa

    rq = rope(r_q.reshape(B, L, RET_HEADS, RET_QK_DIM).transpose(0, 2, 1, 3), pos)
    rk = rope(r_k.reshape(B, L, RET_HEADS, RET_QK_DIM).transpose(0, 2, 1, 3), pos) * (RET_QK_DIM ** -0.5)
    rv = r_v.reshape(B, L, RET_HEADS, RET_V_DIM).transpose(0, 2, 1, 3)
    log_gamma = jnp.log(1.0 - 2.0 ** (-5.0 - jnp.arange(RET_HEADS, dtype=jnp.float32)))
    orr = retention(rq, rk, rv, log_gamma)
    orr = _rms(orr).astype(h.dtype)
    orr = orr.transpose(0, 2, 1, 3).reshape(B, L, RET_V_WIDTH) * jax.nn.silu(r_g)
    y_r = orr @ w_pr

    merged = jax.nn.sigmoid(g_a) * y_a + jax.nn.sigmoid(g_r) * y_r
    h = h + merged @ w_o

    hn = rmsnorm(h, norm2_g)
    h = h + jnp.square(jax.nn.relu(hn @ w_up)) @ w_down
    return h


def setup_inputs(seed: int = 0) -> dict:
    key = jax.random.key(seed)
    ks = jax.random.split(key, 16)
    f32 = jnp.float32

    def w(k, shape, fan_in):
        return jax.random.normal(k, shape, f32) * fan_in ** -0.5

    def gain(k, shape):
        return 1.0 + 0.02 * jax.random.normal(k, shape, f32)

    return {
        'x': jax.random.normal(ks[0], (BATCH, SEQ, D_MODEL), f32),
        'meta_tokens': jax.random.normal(ks[1], (N_META, D_MODEL), f32),
        'norm1_g': gain(ks[2], (DEPTH, D_MODEL)),
        'w_in': w(ks[3], (DEPTH, D_MODEL, IN_WIDTH), D_MODEL),
        'lam_q1': 0.1 * jax.random.normal(ks[4], (DEPTH, DA_HEAD_DIM), f32),
        'lam_k1': 0.1 * jax.random.normal(ks[5], (DEPTH, DA_HEAD_DIM), f32),
        'lam_q2': 0.1 * jax.random.normal(ks[6], (DEPTH, DA_HEAD_DIM), f32),
        'lam_k2': 0.1 * jax.random.normal(ks[7], (DEPTH, DA_HEAD_DIM), f32),
        'da_subln_g': gain(ks[8], (DEPTH, DA_V_DIM)),
        'w_pa': w(ks[9], (DEPTH, DA_V_WIDTH, D_MODEL), DA_V_WIDTH),
        'w_pr': w(ks[10], (DEPTH, RET_V_WIDTH, D_MODEL), RET_V_WIDTH),
        'w_o': w(ks[11], (DEPTH, D_MODEL, D_MODEL), D_MODEL),
        'norm2_g': gain(ks[12], (DEPTH, D_MODEL)),
        'w_up': w(ks[13], (DEPTH, D_MODEL, D_FF), D_MODEL),
        'w_down': w(ks[14], (DEPTH, D_FF, D_MODEL), D_FF),
        'normf_g': gain(ks[15], (D_MODEL,)),
    }


def reference(x, meta_tokens, norm1_g, w_in, lam_q1, lam_k1, lam_q2, lam_k2, da_subln_g,
              w_pa, w_pr, w_o, norm2_g, w_up, w_down, normf_g):
    B, S, D = x.shape
    meta = jnp.broadcast_to(meta_tokens.astype(x.dtype)[None], (B, N_META, D))
    h = jnp.concatenate([meta, x], axis=1)
    pos = jnp.arange(N_META + S)
    for l in range(DEPTH):
        lambda_init = 0.8 - 0.6 * math.exp(-0.3 * l)
        h = hybrid_layer(h, pos, lambda_init, norm1_g[l], w_in[l], lam_q1[l], lam_k1[l],
                         lam_q2[l], lam_k2[l], da_subln_g[l], w_pa[l], w_pr[l], w_o[l],
                         norm2_g[l], w_up[l], w_down[l])
    h = rmsnorm(h, normf_g)
    return h[:, N_META:]
```

```python
import functools
import math

import jax
import jax.numpy as jnp
from jax import lax
from jax.experimental import pallas as pl
from jax.experimental.pallas import tpu as pltpu

F32 = jnp.float32
BF16 = jnp.bfloat16

D_MODEL = 2048
N_META = 16
ROPE_THETA = 10000.0
NORM_EPS = 1e-6
DA_HEADS = 4
DA_HEAD_DIM = 128
DA_V_DIM = 2 * DA_HEAD_DIM
RET_HEADS = 4
RET_QK_DIM = 256
RET_V_DIM = 256
D_FF = 4 * D_MODEL
LAMBDA_INIT = 0.8 - 0.6 * math.exp(-0.3 * 0)

REGION = 1024
COL_DA_Q, COL_DA_K, COL_DA_V, COL_R_Q, COL_R_K, COL_R_V, COL_R_G, COL_G_A, COL_G_R = (
    0, 1, 2, 3, 4, 5, 6, 7, 9)
N_REGIONS = 11
IN_WIDTH = N_REGIONS * REGION

LANES = 128
META_PAD = 128
NEG_BIG = -1e30
VMEM_LIMIT = 56 * 1024 * 1024

NT_DIMS = (((1,), (1,)), ((), ()))
TN_DIMS = (((0,), (0,)), ((), ()))


def _sigmoid(x):
    return 1.0 / (1.0 + jnp.exp(-x))


def _inproj_kernel(x_ref, g_ref, w_ref, tab_ref, o_ref, xn_ref):
    j = pl.program_id(1)

    @pl.when(j == 0)
    def _():
        x = x_ref[...]
        ms = jnp.mean(x * x, axis=-1, keepdims=True)
        xn_ref[...] = (x * lax.rsqrt(ms + NORM_EPS) * g_ref[...]).astype(BF16)

    acc = jnp.dot(xn_ref[...], w_ref[...], preferred_element_type=F32)

    def rope_da(scale):
        cos = tab_ref[:, 0:LANES]
        sin = tab_ref[:, LANES:2 * LANES]
        for c in range(REGION // LANES):
            xc = acc[:, c * LANES:(c + 1) * LANES]
            r = xc * cos + pltpu.roll(xc, LANES // 2, 1) * sin
            o_ref[:, c * LANES:(c + 1) * LANES] = (r * scale).astype(BF16)

    def rope_ret(scale):
        cos = tab_ref[:, 2 * LANES:3 * LANES]
        sin = tab_ref[:, 3 * LANES:4 * LANES]
        for h in range(RET_HEADS):
            lo = h * RET_QK_DIM
            x1 = acc[:, lo:lo + LANES]
            x2 = acc[:, lo + LANES:lo + 2 * LANES]
            o_ref[:, lo:lo + LANES] = ((x1 * cos - x2 * sin) * scale).astype(BF16)
            o_ref[:, lo + LANES:lo + 2 * LANES] = ((x2 * cos + x1 * sin) * scale).astype(BF16)

    @pl.when(j == COL_DA_Q)
    def _():
        rope_da(DA_HEAD_DIM ** -0.5)

    @pl.when(j == COL_DA_K)
    def _():
        rope_da(1.0)

    @pl.when(j == COL_R_Q)
    def _():
        rope_ret(1.0)

    @pl.when(j == COL_R_K)
    def _():
        rope_ret(RET_QK_DIM ** -0.5)

    @pl.when((j == COL_DA_V) | (j == COL_R_V))
    def _():
        o_ref[...] = acc.astype(BF16)

    @pl.when(j == COL_R_G)
    def _():
        o_ref[...] = (acc * _sigmoid(acc)).astype(BF16)

    @pl.when(j >= COL_G_A)
    def _():
        o_ref[...] = _sigmoid(acc).astype(BF16)


def _inproj(x2d, g, w_bf16, tab, tm):
    m = x2d.shape[0]
    n_tab = tab.shape[0] // tm
    return pl.pallas_call(
        _inproj_kernel,
        grid=(m // tm, N_REGIONS),
        in_specs=[
            pl.BlockSpec((tm, D_MODEL), lambda i, j: (i, 0)),
            pl.BlockSpec((1, D_MODEL), lambda i, j: (0, 0)),
            pl.BlockSpec((D_MODEL, REGION), lambda i, j: (0, j)),
            pl.BlockSpec((tm, 4 * LANES), lambda i, j: (i % n_tab, 0)),
        ],
        out_specs=pl.BlockSpec((tm, REGION), lambda i, j: (i, j)),
        out_shape=jax.ShapeDtypeStruct((m, IN_WIDTH), BF16),
        scratch_shapes=[pltpu.VMEM((tm, D_MODEL), BF16)],
        compiler_params=pltpu.CompilerParams(
            dimension_semantics=("arbitrary", "arbitrary"),
            vmem_limit_bytes=VMEM_LIMIT),
        name="inproj",
    )(x2d, g, w_bf16, tab)


def _attn_kernel(lq1_ref, lk1_ref, lq2_ref, lk2_ref, subg_ref, q_ref, k_ref, v_ref,
                 km_ref, vm_ref, o_ref, m_scr, l_scr, acc_scr, *, tq):
    qi = pl.program_id(2)

    m_scr[...] = jnp.full(m_scr.shape, NEG_BIG, F32)
    l_scr[...] = jnp.zeros(l_scr.shape, F32)
    acc_scr[...] = jnp.zeros(acc_scr.shape, F32)

    def update(kblk, vblk, mask):
        for c in range(2):
            qc = q_ref[:, c * DA_HEAD_DIM:(c + 1) * DA_HEAD_DIM]
            kc = kblk[:, c * DA_HEAD_DIM:(c + 1) * DA_HEAD_DIM]
            s = lax.dot_general(qc, kc, NT_DIMS, preferred_element_type=F32)
            if mask is not None:
                s = jnp.where(mask, s, NEG_BIG)
            m_old = m_scr[c]
            m_new = jnp.maximum(m_old, jnp.max(s, axis=-1, keepdims=True))
            alpha = jnp.exp(m_old - m_new)
            p = jnp.exp(s - m_new)
            l_scr[c] = alpha * l_scr[c] + jnp.sum(p, axis=-1, keepdims=True)
            acc_scr[c] = alpha * acc_scr[c] + jnp.dot(
                p.astype(BF16), vblk, preferred_element_type=F32)
            m_scr[c] = m_new

    meta_mask = lax.broadcasted_iota(jnp.int32, (tq, META_PAD), 1) < N_META
    update(km_ref[...], vm_ref[...], meta_mask)

    def body(j, carry):
        start = pl.multiple_of(j * tq, tq)
        update(k_ref[pl.ds(start, tq), :], v_ref[pl.ds(start, tq), :], None)
        return carry

    lax.fori_loop(0, qi, body, 0)

    start = pl.multiple_of(qi * tq, tq)
    row = lax.broadcasted_iota(jnp.int32, (tq, tq), 0)
    col = lax.broadcasted_iota(jnp.int32, (tq, tq), 1)
    update(k_ref[pl.ds(start, tq), :], v_ref[pl.ds(start, tq), :], col <= row)

    lam = (jnp.exp(jnp.sum(lq1_ref[...] * lk1_ref[...], axis=-1, keepdims=True))
           - jnp.exp(jnp.sum(lq2_ref[...] * lk2_ref[...], axis=-1, keepdims=True))
           + LAMBDA_INIT)
    o = acc_scr[0] / l_scr[0] - lam * (acc_scr[1] / l_scr[1])
    ms = jnp.mean(o * o, axis=-1, keepdims=True)
    o = o * lax.rsqrt(ms + NORM_EPS) * subg_ref[...] * (1.0 - LAMBDA_INIT)
    o_ref[...] = o.astype(BF16)


def _diff_attention(proj, meta_k, meta_v, lq1, lk1, lq2, lk2, subg, batch, seq, tq):
    nq = seq // tq
    vec = pl.BlockSpec((1, DA_HEAD_DIM), lambda b, h, i: (0, 0))
    per_head = REGION // DA_V_DIM
    return pl.pallas_call(
        functools.partial(_attn_kernel, tq=tq),
        grid=(batch, DA_HEADS, nq),
        in_specs=[
            vec, vec, vec, vec,
            pl.BlockSpec((1, DA_V_DIM), lambda b, h, i: (0, 0)),
            pl.BlockSpec((tq, DA_V_DIM), lambda b, h, i: (b * nq + i, COL_DA_Q * per_head + h)),
            pl.BlockSpec((seq, DA_V_DIM), lambda b, h, i: (b, COL_DA_K * per_head + h)),
            pl.BlockSpec((seq, DA_V_DIM), lambda b, h, i: (b, COL_DA_V * per_head + h)),
            pl.BlockSpec((META_PAD, DA_V_DIM), lambda b, h, i: (0, h)),
            pl.BlockSpec((META_PAD, DA_V_DIM), lambda b, h, i: (0, h)),
        ],
        out_specs=pl.BlockSpec((tq, DA_V_DIM), lambda b, h, i: (b * nq + i, h)),
        out_shape=jax.ShapeDtypeStruct((batch * seq, DA_HEADS * DA_V_DIM), BF16),
        scratch_shapes=[
            pltpu.VMEM((2, tq, 1), F32),
            pltpu.VMEM((2, tq, 1), F32),
            pltpu.VMEM((2, tq, DA_V_DIM), F32),
        ],
        compiler_params=pltpu.CompilerParams(
            dimension_semantics=("arbitrary", "arbitrary", "arbitrary"),
            vmem_limit_bytes=VMEM_LIMIT),
        name="diff_attention",
    )(lq1, lk1, lq2, lk2, subg, proj, proj, proj, meta_k, meta_v)


def _ret_kernel(lg_ref, q_ref, k_ref, v_ref, g_ref, km_ref, vm_ref, o_ref,
                state_scr, decay_scr, xi_scr, zeta_scr, *, chunk):
    h = pl.program_id(1)
    c = pl.program_id(2)
    lg = lg_ref[h]

    @pl.when(c == 0)
    def _():
        row = lax.broadcasted_iota(jnp.int32, (chunk, chunk), 0)
        col = lax.broadcasted_iota(jnp.int32, (chunk, chunk), 1)
        rel = (row - col).astype(F32)
        decay_scr[...] = jnp.where(rel >= 0, jnp.exp(lg * jnp.maximum(rel, 0.0)), 0.0)
        idx = lax.broadcasted_iota(jnp.int32, (chunk, 1), 0).astype(F32)
        xi_scr[...] = jnp.exp(lg * (idx + 1.0))
        zeta_scr[...] = jnp.exp(lg * (chunk - 1.0 - idx))
        midx = lax.broadcasted_iota(jnp.int32, (META_PAD, 1), 0).astype(F32)
        mz = jnp.exp(lg * (N_META - 1.0 - midx))
        vz = (vm_ref[...].astype(F32) * mz).astype(BF16)
        state_scr[...] = lax.dot_general(km_ref[...], vz, TN_DIMS, preferred_element_type=F32)

    q = q_ref[...]
    k = k_ref[...]
    v = v_ref[...]
    s = lax.dot_general(q, k, NT_DIMS, preferred_element_type=F32) * decay_scr[...]
    inner = jnp.dot(s.astype(BF16), v, preferred_element_type=F32)
    state = state_scr[...]
    cross = jnp.dot(q, state.astype(BF16), preferred_element_type=F32) * xi_scr[...]
    o = inner + cross
    vz = (v.astype(F32) * zeta_scr[...]).astype(BF16)
    chunk_decay = jnp.exp(jnp.full((1, 1), lg * chunk, F32))
    state_scr[...] = chunk_decay * state + lax.dot_general(
        k, vz, TN_DIMS, preferred_element_type=F32)

    ms = jnp.mean(o * o, axis=-1, keepdims=True)
    o_ref[...] = (o * lax.rsqrt(ms + NORM_EPS) * g_ref[...].astype(F32)).astype(BF16)


def _retention(log_gamma, proj, meta_k, meta_v, batch, seq, chunk):
    nc = seq // chunk
    per_head = REGION // RET_V_DIM

    def tok(col):
        return pl.BlockSpec((chunk, RET_V_DIM),
                            lambda b, h, c, lg: (b * nc + c, col * per_head + h))

    meta = pl.BlockSpec((META_PAD, RET_V_DIM), lambda b, h, c, lg: (0, h))
    grid_spec = pltpu.PrefetchScalarGridSpec(
        num_scalar_prefetch=1,
        grid=(batch, RET_HEADS, nc),
        in_specs=[tok(COL_R_Q), tok(COL_R_K), tok(COL_R_V), tok(COL_R_G), meta, meta],
        out_specs=pl.BlockSpec((chunk, RET_V_DIM), lambda b, h, c, lg: (b * nc + c, h)),
        scratch_shapes=[
            pltpu.VMEM((RET_QK_DIM, RET_V_DIM), F32),
            pltpu.VMEM((chunk, chunk), F32),
            pltpu.VMEM((chunk, 1), F32),
            pltpu.VMEM((chunk, 1), F32),
        ],
    )
    return pl.pallas_call(
        functools.partial(_ret_kernel, chunk=chunk),
        grid_spec=grid_spec,
        out_shape=jax.ShapeDtypeStruct((batch * seq, RET_HEADS * RET_V_DIM), BF16),
        compiler_params=pltpu.CompilerParams(
            dimension_semantics=("arbitrary", "arbitrary", "arbitrary"),
            vmem_limit_bytes=VMEM_LIMIT),
        name="retention",
    )(log_gamma, proj, proj, proj, proj, meta_k, meta_v)


def _merge_kernel(oa_ref, or_ref, ga0_ref, ga1_ref, gr0_ref, gr1_ref, x_ref,
                  wpa_ref, wpr_ref, wo_ref, g2_ref, h_ref, hn_ref):
    oa = oa_ref[...]
    orr = or_ref[...]
    h = x_ref[...]
    for n, (ga_ref, gr_ref) in enumerate(((ga0_ref, gr0_ref), (ga1_ref, gr1_ref))):
        cols = slice(n * REGION, (n + 1) * REGION)
        ya = jnp.dot(oa, wpa_ref[:, cols], preferred_element_type=F32)
        yr = jnp.dot(orr, wpr_ref[:, cols], preferred_element_type=F32)
        merged = ga_ref[...].astype(F32) * ya + gr_ref[...].astype(F32) * yr
        h = h + jnp.dot(merged.astype(BF16), wo_ref[cols, :], preferred_element_type=F32)
    h_ref[...] = h
    ms = jnp.mean(h * h, axis=-1, keepdims=True)
    hn_ref[...] = (h * lax.rsqrt(ms + NORM_EPS) * g2_ref[...]).astype(BF16)


def _merge(oa, orr, proj, x2d, wpa, wpr, wo, g2, tm):
    m = x2d.shape[0]
    row = lambda width: pl.BlockSpec((tm, width), lambda i: (i, 0))
    gate = lambda col: pl.BlockSpec((tm, REGION), lambda i: (i, col))
    whole = lambda shape: pl.BlockSpec(shape, lambda i: (0, 0), pipeline_mode=pl.Buffered(1))
    return pl.pallas_call(
        _merge_kernel,
        grid=(m // tm,),
        in_specs=[
            row(REGION), row(REGION),
            gate(COL_G_A), gate(COL_G_A + 1), gate(COL_G_R), gate(COL_G_R + 1),
            row(D_MODEL),
            whole((REGION, D_MODEL)), whole((REGION, D_MODEL)), whole((D_MODEL, D_MODEL)),
            whole((1, D_MODEL)),
        ],
        out_specs=[row(D_MODEL), row(D_MODEL)],
        out_shape=[jax.ShapeDtypeStruct((m, D_MODEL), F32),
                   jax.ShapeDtypeStruct((m, D_MODEL), BF16)],
        compiler_params=pltpu.CompilerParams(
            dimension_semantics=("arbitrary",),
            vmem_limit_bytes=VMEM_LIMIT),
        name="merge",
    )(oa, orr, proj, proj, proj, proj, x2d, wpa, wpr, wo, g2)


def _mlp_kernel(hn_ref, wup_ref, wdown_ref, h_ref, gf_ref, o_ref):
    f = pl.program_id(1)
    u = jnp.dot(hn_ref[...], wup_ref[...], preferred_element_type=F32)
    a = jnp.square(jnp.maximum(u, 0.0)).astype(BF16)
    d = jnp.dot(a, wdown_ref[...], preferred_element_type=F32)

    @pl.when(f == 0)
    def _():
        o_ref[...] = h_ref[...] + d

    @pl.when(f > 0)
    def _():
        o_ref[...] += d

    @pl.when(f == pl.num_programs(1) - 1)
    def _():
        y = o_ref[...]
        ms = jnp.mean(y * y, axis=-1, keepdims=True)
        o_ref[...] = y * lax.rsqrt(ms + NORM_EPS) * gf_ref[...]


def _mlp(hn, h1, wup, wdown, gf, tm, tf):
    m = hn.shape[0]
    return pl.pallas_call(
        _mlp_kernel,
        grid=(m // tm, D_FF // tf),
        in_specs=[
            pl.BlockSpec((tm, D_MODEL), lambda i, f: (i, 0)),
            pl.BlockSpec((D_MODEL, tf), lambda i, f: (0, f)),
            pl.BlockSpec((tf, D_MODEL), lambda i, f: (f, 0)),
            pl.BlockSpec((tm, D_MODEL), lambda i, f: (i, 0)),
            pl.BlockSpec((1, D_MODEL), lambda i, f: (0, 0)),
        ],
        out_specs=pl.BlockSpec((tm, D_MODEL), lambda i, f: (i, 0)),
        out_shape=jax.ShapeDtypeStruct((m, D_MODEL), F32),
        compiler_params=pltpu.CompilerParams(
            dimension_semantics=("arbitrary", "arbitrary"),
            vmem_limit_bytes=VMEM_LIMIT),
        name="mlp",
    )(hn, wup, wdown, h1, gf)


def _rope_table(pos):
    pos = pos.astype(F32)[:, None]

    def cs(half):
        inv = ROPE_THETA ** (-jnp.arange(half, dtype=F32) / half)
        ang = pos * inv[None, :]
        return jnp.cos(ang), jnp.sin(ang)

    cos_a, sin_a = cs(DA_HEAD_DIM // 2)
    cos_r, sin_r = cs(RET_QK_DIM // 2)
    return jnp.concatenate([cos_a, cos_a, -sin_a, sin_a, cos_r, sin_r], axis=-1)


def kernel(x, meta_tokens, norm1_g, w_in, lam_q1, lam_k1, lam_q2, lam_k2, da_subln_g,
           w_pa, w_pr, w_o, norm2_g, w_up, w_down, normf_g):
    batch, seq, d = x.shape
    x2d = x.reshape(batch * seq, d)
    w_in_b = w_in[0].astype(BF16)
    g1 = norm1_g[0].reshape(1, d)

    tab_real = _rope_table(N_META + jnp.arange(seq))
    tab_meta = _rope_table(jnp.arange(N_META))

    proj = _inproj(x2d, g1, w_in_b, tab_real, tm=1024)
    proj_meta = _inproj(meta_tokens.astype(F32), g1, w_in_b, tab_meta, tm=N_META)
    pad = ((0, META_PAD - N_META), (0, 0))
    sl = lambda col: jnp.pad(proj_meta[:, col * REGION:(col + 1) * REGION], pad)

    row = lambda a: a[0].reshape(1, -1)
    oa = _diff_attention(proj, sl(COL_DA_K), sl(COL_DA_V), row(lam_q1), row(lam_k1),
                         row(lam_q2), row(lam_k2), row(da_subln_g), batch, seq, tq=512)

    log_gamma = jnp.log(1.0 - 2.0 ** (-5.0 - jnp.arange(RET_HEADS, dtype=F32)))
    orr = _retention(log_gamma, proj, sl(COL_R_K), sl(COL_R_V), batch, seq, chunk=256)

    h1, hn = _merge(oa, orr, proj, x2d, w_pa[0].astype(BF16), w_pr[0].astype(BF16),
                    w_o[0].astype(BF16), row(norm2_g), tm=512)
    out = _mlp(hn, h1, w_up[0].astype(BF16), w_down[0].astype(BF16),
               normf_g.reshape(1, d), tm=512, tf=1024)
    return out.reshape(batch, seq, d)
```

```python
import functools
import math

import jax
import jax.numpy as jnp
from jax import lax
from jax.experimental import pallas as pl
from jax.experimental.pallas import tpu as pltpu

F32 = jnp.float32
BF16 = jnp.bfloat16

D_MODEL = 2048
N_META = 16
ROPE_THETA = 10000.0
NORM_EPS = 1e-6
DA_HEADS = 4
DA_HEAD_DIM = 128
DA_V_DIM = 2 * DA_HEAD_DIM
RET_HEADS = 4
RET_QK_DIM = 256
RET_V_DIM = 256
D_FF = 4 * D_MODEL
LAMBDA_INIT = 0.8 - 0.6 * math.exp(-0.3 * 0)

REGION = 1024
COL_DA_Q, COL_DA_K, COL_DA_V, COL_R_Q, COL_R_K, COL_R_V, COL_R_G, COL_G_A, COL_G_R = (
    0, 1, 2, 3, 4, 5, 6, 7, 9)
N_REGIONS = 11
IN_WIDTH = N_REGIONS * REGION

LANES = 128
META_PAD = 128
NEG_BIG = -1e30
LOG2_E = math.log2(math.e)
VMEM_LIMIT = 56 * 1024 * 1024

NT_DIMS = (((1,), (1,)), ((), ()))
TN_DIMS = (((0,), (0,)), ((), ()))


def _sigmoid(x):
    return 1.0 / (1.0 + jnp.exp(-x))


def _inproj_kernel(x_ref, g_ref, w_ref, tab_ref, o_ref, xn_ref):
    j = pl.program_id(1)

    @pl.when(j == 0)
    def _():
        x = x_ref[...]
        ms = jnp.mean(x * x, axis=-1, keepdims=True)
        xn_ref[...] = (x * lax.rsqrt(ms + NORM_EPS) * g_ref[...]).astype(BF16)

    def project():
        return jnp.dot(xn_ref[...], w_ref[...], preferred_element_type=F32)

    def scaled(t, scale):
        return t if scale == 1.0 else t * scale

    def rope_da(scale):
        acc = project()
        cos = scaled(tab_ref[:, 0:LANES], scale)
        sin = scaled(tab_ref[:, LANES:2 * LANES], scale)
        for c in range(REGION // LANES):
            xc = acc[:, c * LANES:(c + 1) * LANES]
            r = xc * cos + pltpu.roll(xc, LANES // 2, 1) * sin
            o_ref[:, c * LANES:(c + 1) * LANES] = r.astype(BF16)

    def rope_ret(scale):
        acc = project()
        cos = scaled(tab_ref[:, 2 * LANES:3 * LANES], scale)
        sin = scaled(tab_ref[:, 3 * LANES:4 * LANES], scale)
        for h in range(RET_HEADS):
            lo = h * RET_QK_DIM
            x1 = acc[:, lo:lo + LANES]
            x2 = acc[:, lo + LANES:lo + 2 * LANES]
            o_ref[:, lo:lo + LANES] = (x1 * cos - x2 * sin).astype(BF16)
            o_ref[:, lo + LANES:lo + 2 * LANES] = (x2 * cos + x1 * sin).astype(BF16)

    @pl.when(j == COL_DA_Q)
    def _():
        rope_da(DA_HEAD_DIM ** -0.5 * LOG2_E)

    @pl.when(j == COL_DA_K)
    def _():
        rope_da(1.0)

    @pl.when(j == COL_R_Q)
    def _():
        rope_ret(1.0)

    @pl.when(j == COL_R_K)
    def _():
        rope_ret(RET_QK_DIM ** -0.5)

    @pl.when((j == COL_DA_V) | (j == COL_R_V))
    def _():
        o_ref[...] = project().astype(BF16)

    @pl.when(j == COL_R_G)
    def _():
        acc = project()
        o_ref[...] = (acc * _sigmoid(acc)).astype(BF16)

    @pl.when(j >= COL_G_A)
    def _():
        o_ref[...] = _sigmoid(project()).astype(BF16)


def _inproj(x2d, g, w_bf16, tab, tm):
    m = x2d.shape[0]
    n_tab = tab.shape[0] // tm
    return pl.pallas_call(
        _inproj_kernel,
        grid=(m // tm, N_REGIONS),
        in_specs=[
            pl.BlockSpec((tm, D_MODEL), lambda i, j: (i, 0)),
            pl.BlockSpec((1, D_MODEL), lambda i, j: (0, 0)),
            pl.BlockSpec((D_MODEL, REGION), lambda i, j: (0, j)),
            pl.BlockSpec((tm, 4 * LANES), lambda i, j: (i % n_tab, 0)),
        ],
        out_specs=pl.BlockSpec((tm, REGION), lambda i, j: (i, j)),
        out_shape=jax.ShapeDtypeStruct((m, IN_WIDTH), BF16),
        scratch_shapes=[pltpu.VMEM((tm, D_MODEL), BF16)],
        compiler_params=pltpu.CompilerParams(
            dimension_semantics=("arbitrary", "arbitrary"),
            vmem_limit_bytes=VMEM_LIMIT),
        name="inproj",
    )(x2d, g, w_bf16, tab)


def _attn_kernel(lq1_ref, lk1_ref, lq2_ref, lk2_ref, subg_ref, q_ref, k_ref, v_ref,
                 km_ref, vm_ref, o_ref, m_scr, l_scr, acc_scr, *, tq):
    qi = pl.program_id(2)

    def update(k_src, v_src, rows, mask, first):
        vblk = v_src[rows, :]
        for c in range(2):
            lanes = slice(c * DA_HEAD_DIM, (c + 1) * DA_HEAD_DIM)
            s = lax.dot_general(q_ref[:, lanes], k_src[rows, lanes], NT_DIMS,
                                preferred_element_type=F32)
            if mask is not None:
                s = jnp.where(mask, s, NEG_BIG)
            chunks = [s[:, t * LANES:(t + 1) * LANES] for t in range(s.shape[1] // LANES)]
            m_cur = jnp.max(functools.reduce(jnp.maximum, chunks), axis=-1, keepdims=True)
            if first:
                m_new = jnp.broadcast_to(m_cur, (tq, LANES))
            else:
                m_old = m_scr[c]
                m_new = jnp.maximum(m_old, m_cur)
                alpha = jnp.exp2(m_old - m_new)
            ps = [jnp.exp2(ch - m_new) for ch in chunks]
            psum = functools.reduce(jnp.add, ps)
            p = ps[0] if len(ps) == 1 else jnp.concatenate(ps, axis=1)
            pv = jnp.dot(p.astype(BF16), vblk, preferred_element_type=F32)
            if first:
                l_scr[c] = psum
                acc_scr[c] = pv
            else:
                l_scr[c] = alpha * l_scr[c] + psum
                acc_scr[c] = jnp.concatenate([alpha] * (DA_V_DIM // LANES), axis=1) * acc_scr[c] + pv
            m_scr[c] = m_new

    meta_mask = lax.broadcasted_iota(jnp.int32, (tq, META_PAD), 1) < N_META
    update(km_ref, vm_ref, slice(None), meta_mask, first=True)

    def body(j, carry):
        update(k_ref, v_ref, pl.ds(pl.multiple_of(j * tq, tq), tq), None, first=False)
        return carry

    lax.fori_loop(0, qi, body, 0)

    row = lax.broadcasted_iota(jnp.int32, (tq, tq), 0)
    col = lax.broadcasted_iota(jnp.int32, (tq, tq), 1)
    update(k_ref, v_ref, pl.ds(pl.multiple_of(qi * tq, tq), tq), col <= row, first=False)

    lam = (jnp.exp(jnp.sum(lq1_ref[...] * lk1_ref[...], axis=-1, keepdims=True))
           - jnp.exp(jnp.sum(lq2_ref[...] * lk2_ref[...], axis=-1, keepdims=True))
           + LAMBDA_INIT)
    l1 = jnp.sum(l_scr[0], axis=-1, keepdims=True)
    l2 = jnp.sum(l_scr[1], axis=-1, keepdims=True)
    o = acc_scr[0] / l1 - lam * (acc_scr[1] / l2)
    ms = jnp.mean(o * o, axis=-1, keepdims=True)
    o = o * lax.rsqrt(ms + NORM_EPS) * subg_ref[...] * (1.0 - LAMBDA_INIT)
    o_ref[...] = o.astype(BF16)


def _diff_attention(proj, meta_k, meta_v, lq1, lk1, lq2, lk2, subg, batch, seq, tq):
    nq = seq // tq
    vec = pl.BlockSpec((1, DA_HEAD_DIM), lambda b, h, i: (0, 0))
    per_head = REGION // DA_V_DIM
    return pl.pallas_call(
        functools.partial(_attn_kernel, tq=tq),
        grid=(batch, DA_HEADS, nq),
        in_specs=[
            vec, vec, vec, vec,
            pl.BlockSpec((1, DA_V_DIM), lambda b, h, i: (0, 0)),
            pl.BlockSpec((tq, DA_V_DIM), lambda b, h, i: (b * nq + i, COL_DA_Q * per_head + h)),
            pl.BlockSpec((seq, DA_V_DIM), lambda b, h, i: (b, COL_DA_K * per_head + h)),
            pl.BlockSpec((seq, DA_V_DIM), lambda b, h, i: (b, COL_DA_V * per_head + h)),
            pl.BlockSpec((META_PAD, DA_V_DIM), lambda b, h, i: (0, h)),
            pl.BlockSpec((META_PAD, DA_V_DIM), lambda b, h, i: (0, h)),
        ],
        out_specs=pl.BlockSpec((tq, DA_V_DIM), lambda b, h, i: (b * nq + i, h)),
        out_shape=jax.ShapeDtypeStruct((batch * seq, DA_HEADS * DA_V_DIM), BF16),
        scratch_shapes=[
            pltpu.VMEM((2, tq, LANES), F32),
            pltpu.VMEM((2, tq, LANES), F32),
            pltpu.VMEM((2, tq, DA_V_DIM), F32),
        ],
        compiler_params=pltpu.CompilerParams(
            dimension_semantics=("arbitrary", "arbitrary", "arbitrary"),
            vmem_limit_bytes=VMEM_LIMIT),
        name="diff_attention",
    )(lq1, lk1, lq2, lk2, subg, proj, proj, proj, meta_k, meta_v)


def _ret_kernel(lg_ref, q_ref, k_ref, v_ref, g_ref, km_ref, vm_ref, o_ref,
                state_scr, decay_scr, xi_scr, zeta_scr, *, chunk):
    h = pl.program_id(1)
    c = pl.program_id(2)
    lg = lg_ref[h]

    @pl.when(c == 0)
    def _():
        row = lax.broadcasted_iota(jnp.int32, (chunk, chunk), 0)
        col = lax.broadcasted_iota(jnp.int32, (chunk, chunk), 1)
        rel = (row - col).astype(F32)
        decay_scr[...] = jnp.where(rel >= 0, jnp.exp(lg * jnp.maximum(rel, 0.0)), 0.0)
        idx = lax.broadcasted_iota(jnp.int32, (chunk, 1), 0).astype(F32)
        xi_scr[...] = jnp.exp(lg * (idx + 1.0))
        zeta_scr[...] = jnp.exp(lg * (chunk - 1.0 - idx))
        midx = lax.broadcasted_iota(jnp.int32, (META_PAD, 1), 0).astype(F32)
        mz = jnp.exp(lg * (N_META - 1.0 - midx))
        vz = (vm_ref[...].astype(F32) * mz).astype(BF16)
        state_scr[...] = lax.dot_general(km_ref[...], vz, TN_DIMS, preferred_element_type=F32)

    q = q_ref[...]
    k = k_ref[...]
    v = v_ref[...]
    s = lax.dot_general(q, k, NT_DIMS, preferred_element_type=F32) * decay_scr[...]
    inner = jnp.dot(s.astype(BF16), v, preferred_element_type=F32)
    state = state_scr[...]
    cross = jnp.dot(q, state.astype(BF16), preferred_element_type=F32) * xi_scr[...]
    o = inner + cross
    vz = (v.astype(F32) * zeta_scr[...]).astype(BF16)
    chunk_decay = jnp.exp(jnp.full((1, 1), lg * chunk, F32))
    state_scr[...] = chunk_decay * state + lax.dot_general(
        k, vz, TN_DIMS, preferred_element_type=F32)

    ms = jnp.mean(o * o, axis=-1, keepdims=True)
    o_ref[...] = (o * lax.rsqrt(ms + NORM_EPS) * g_ref[...].astype(F32)).astype(BF16)


def _retention(log_gamma, proj, meta_k, meta_v, batch, seq, chunk):
    nc = seq // chunk
    per_head = REGION // RET_V_DIM

    def tok(col):
        return pl.BlockSpec((chunk, RET_V_DIM),
                            lambda b, h, c, lg: (b * nc + c, col * per_head + h))

    meta = pl.BlockSpec((META_PAD, RET_V_DIM), lambda b, h, c, lg: (0, h))
    grid_spec = pltpu.PrefetchScalarGridSpec(
        num_scalar_prefetch=1,
        grid=(batch, RET_HEADS, nc),
        in_specs=[tok(COL_R_Q), tok(COL_R_K), tok(COL_R_V), tok(COL_R_G), meta, meta],
        out_specs=pl.BlockSpec((chunk, RET_V_DIM), lambda b, h, c, lg: (b * nc + c, h)),
        scratch_shapes=[
            pltpu.VMEM((RET_QK_DIM, RET_V_DIM), F32),
            pltpu.VMEM((chunk, chunk), F32),
            pltpu.VMEM((chunk, 1), F32),
            pltpu.VMEM((chunk, 1), F32),
        ],
    )
    return pl.pallas_call(
        functools.partial(_ret_kernel, chunk=chunk),
        grid_spec=grid_spec,
        out_shape=jax.ShapeDtypeStruct((batch * seq, RET_HEADS * RET_V_DIM), BF16),
        compiler_params=pltpu.CompilerParams(
            dimension_semantics=("arbitrary", "arbitrary", "arbitrary"),
            vmem_limit_bytes=VMEM_LIMIT),
        name="retention",
    )(log_gamma, proj, proj, proj, proj, meta_k, meta_v)


def _merge_kernel(oa_ref, or_ref, ga0_ref, ga1_ref, gr0_ref, gr1_ref, x_ref,
                  wpa_ref, wpr_ref, wo_ref, g2_ref, h_ref, hn_ref):
    oa = oa_ref[...]
    orr = or_ref[...]
    h = x_ref[...]
    for n, (ga_ref, gr_ref) in enumerate(((ga0_ref, gr0_ref), (ga1_ref, gr1_ref))):
        cols = slice(n * REGION, (n + 1) * REGION)
        ya = jnp.dot(oa, wpa_ref[:, cols], preferred_element_type=F32)
        yr = jnp.dot(orr, wpr_ref[:, cols], preferred_element_type=F32)
        merged = ga_ref[...].astype(F32) * ya + gr_ref[...].astype(F32) * yr
        h = h + jnp.dot(merged.astype(BF16), wo_ref[cols, :], preferred_element_type=F32)
    h_ref[...] = h
    ms = jnp.mean(h * h, axis=-1, keepdims=True)
    hn_ref[...] = (h * lax.rsqrt(ms + NORM_EPS) * g2_ref[...]).astype(BF16)


def _merge(oa, orr, proj, x2d, wpa, wpr, wo, g2, tm):
    m = x2d.shape[0]
    row = lambda width: pl.BlockSpec((tm, width), lambda i: (i, 0))
    gate = lambda col: pl.BlockSpec((tm, REGION), lambda i: (i, col))
    whole = lambda shape: pl.BlockSpec(shape, lambda i: (0, 0), pipeline_mode=pl.Buffered(1))
    return pl.pallas_call(
        _merge_kernel,
        grid=(m // tm,),
        in_specs=[
            row(REGION), row(REGION),
            gate(COL_G_A), gate(COL_G_A + 1), gate(COL_G_R), gate(COL_G_R + 1),
            row(D_MODEL),
            whole((REGION, D_MODEL)), whole((REGION, D_MODEL)), whole((D_MODEL, D_MODEL)),
            whole((1, D_MODEL)),
        ],
        out_specs=[row(D_MODEL), row(D_MODEL)],
        out_shape=[jax.ShapeDtypeStruct((m, D_MODEL), F32),
                   jax.ShapeDtypeStruct((m, D_MODEL), BF16)],
        compiler_params=pltpu.CompilerParams(
            dimension_semantics=("arbitrary",),
            vmem_limit_bytes=VMEM_LIMIT),
        name="merge",
    )(oa, orr, proj, proj, proj, proj, x2d, wpa, wpr, wo, g2)


def _mlp_kernel(hn_ref, wup_ref, wdown_ref, h_ref, gf_ref, o_ref):
    f = pl.program_id(1)
    last = pl.num_programs(1) - 1

    def ffn_chunk():
        u = jnp.dot(hn_ref[...], wup_ref[...], preferred_element_type=F32)
        a = jnp.square(jnp.maximum(u, 0.0)).astype(BF16)
        return jnp.dot(a, wdown_ref[...], preferred_element_type=F32)

    @pl.when(f == 0)
    def _():
        o_ref[...] = h_ref[...] + ffn_chunk()

    @pl.when((f > 0) & (f < last))
    def _():
        o_ref[...] += ffn_chunk()

    @pl.when(f == last)
    def _():
        y = o_ref[...] + ffn_chunk()
        ms = jnp.mean(y * y, axis=-1, keepdims=True)
        o_ref[...] = y * lax.rsqrt(ms + NORM_EPS) * gf_ref[...]


def _mlp(hn, h1, wup, wdown, gf, tm, tf):
    m = hn.shape[0]
    return pl.pallas_call(
        _mlp_kernel,
        grid=(m // tm, D_FF // tf),
        in_specs=[
            pl.BlockSpec((tm, D_MODEL), lambda i, f: (i, 0)),
            pl.BlockSpec((D_MODEL, tf), lambda i, f: (0, f)),
            pl.BlockSpec((tf, D_MODEL), lambda i, f: (f, 0)),
            pl.BlockSpec((tm, D_MODEL), lambda i, f: (i, 0)),
            pl.BlockSpec((1, D_MODEL), lambda i, f: (0, 0)),
        ],
        out_specs=pl.BlockSpec((tm, D_MODEL), lambda i, f: (i, 0)),
        out_shape=jax.ShapeDtypeStruct((m, D_MODEL), F32),
        compiler_params=pltpu.CompilerParams(
            dimension_semantics=("arbitrary", "arbitrary"),
            vmem_limit_bytes=VMEM_LIMIT),
        name="mlp",
    )(hn, wup, wdown, h1, gf)


def _rope_table(pos):
    pos = pos.astype(F32)[:, None]

    def cs(half):
        inv = ROPE_THETA ** (-jnp.arange(half, dtype=F32) / half)
        ang = pos * inv[None, :]
        return jnp.cos(ang), jnp.sin(ang)

    cos_a, sin_a = cs(DA_HEAD_DIM // 2)
    cos_r, sin_r = cs(RET_QK_DIM // 2)
    return jnp.concatenate([cos_a, cos_a, -sin_a, sin_a, cos_r, sin_r], axis=-1)


def kernel(x, meta_tokens, norm1_g, w_in, lam_q1, lam_k1, lam_q2, lam_k2, da_subln_g,
           w_pa, w_pr, w_o, norm2_g, w_up, w_down, normf_g):
    batch, seq, d = x.shape
    x2d = x.reshape(batch * seq, d)
    w_in_b = w_in[0].astype(BF16)
    g1 = norm1_g[0].reshape(1, d)

    tab_real = _rope_table(N_META + jnp.arange(seq))
    tab_meta = _rope_table(jnp.arange(N_META))

    proj = _inproj(x2d, g1, w_in_b, tab_real, tm=1024)
    proj_meta = _inproj(meta_tokens.astype(F32), g1, w_in_b, tab_meta, tm=N_META)
    pad = ((0, META_PAD - N_META), (0, 0))
    sl = lambda col: jnp.pad(proj_meta[:, col * REGION:(col + 1) * REGION], pad)

    row = lambda a: a[0].reshape(1, -1)
    oa = _diff_attention(proj, sl(COL_DA_K), sl(COL_DA_V), row(lam_q1), row(lam_k1),
                         row(lam_q2), row(lam_k2), row(da_subln_g), batch, seq, tq=512)

    log_gamma = jnp.log(1.0 - 2.0 ** (-5.0 - jnp.arange(RET_HEADS, dtype=F32)))
    orr = _retention(log_gamma, proj, sl(COL_R_K), sl(COL_R_V), batch, seq, chunk=256)

    h1, hn = _merge(oa, orr, proj, x2d, w_pa[0].astype(BF16), w_pr[0].astype(BF16),
                    w_o[0].astype(BF16), row(norm2_g), tm=512)
    out = _mlp(hn, h1, w_up[0].astype(BF16), w_down[0].astype(BF16),
               normf_g.reshape(1, d), tm=512, tf=1024)
    return out.reshape(batch, seq, d)
```

```python
import functools
import math

import jax
import jax.numpy as jnp
from jax import lax
from jax.experimental import pallas as pl
from jax.experimental.pallas import tpu as pltpu

F32 = jnp.float32
BF16 = jnp.bfloat16

D_MODEL = 2048
N_META = 16
ROPE_THETA = 10000.0
NORM_EPS = 1e-6
DA_HEADS = 4
DA_HEAD_DIM = 128
DA_V_DIM = 2 * DA_HEAD_DIM
RET_HEADS = 4
RET_QK_DIM = 256
RET_V_DIM = 256
D_FF = 4 * D_MODEL
LAMBDA_INIT = 0.8 - 0.6 * math.exp(-0.3 * 0)

REGION = 1024
COL_DA_Q, COL_DA_K, COL_DA_V, COL_R_Q, COL_R_K, COL_R_V, COL_R_G, COL_G_A, COL_G_R = (
    0, 1, 2, 3, 4, 5, 6, 7, 9)
N_REGIONS = 11
IN_WIDTH = N_REGIONS * REGION

LANES = 128
META_PAD = 128
NEG_BIG = -1e30
LOG2_E = math.log2(math.e)
VMEM_LIMIT = 56 * 1024 * 1024

NT_DIMS = (((1,), (1,)), ((), ()))
TN_DIMS = (((0,), (0,)), ((), ()))


def _sigmoid(x):
    return 1.0 / (1.0 + jnp.exp(-x))


def _region_id(j, regions):
    if regions == tuple(range(len(regions))):
        return j
    r = regions[0]
    for idx, region in enumerate(regions[1:], 1):
        r = jnp.where(j >= idx, region, r)
    return r


def _inproj_kernel(x_ref, g_ref, w_ref, tab_ref, o_ref, xn_ref, *, regions):
    first = pl.program_id(1) == 0
    j = _region_id(pl.program_id(1), regions)

    @pl.when(first)
    def _():
        x = x_ref[...]
        ms = jnp.mean(x * x, axis=-1, keepdims=True)
        xn_ref[...] = (x * lax.rsqrt(ms + NORM_EPS) * g_ref[...]).astype(BF16)

    def project():
        return jnp.dot(xn_ref[...], w_ref[...], preferred_element_type=F32)

    def scaled(t, scale):
        return t if scale == 1.0 else t * scale

    def rope_da(scale):
        acc = project()
        cos = scaled(tab_ref[:, 0:LANES], scale)
        sin = scaled(tab_ref[:, LANES:2 * LANES], scale)
        for c in range(REGION // LANES):
            xc = acc[:, c * LANES:(c + 1) * LANES]
            r = xc * cos + pltpu.roll(xc, LANES // 2, 1) * sin
            o_ref[:, c * LANES:(c + 1) * LANES] = r.astype(BF16)

    def rope_ret(scale):
        acc = project()
        cos = scaled(tab_ref[:, 2 * LANES:3 * LANES], scale)
        sin = scaled(tab_ref[:, 3 * LANES:4 * LANES], scale)
        for h in range(RET_HEADS):
            lo = h * RET_QK_DIM
            x1 = acc[:, lo:lo + LANES]
            x2 = acc[:, lo + LANES:lo + 2 * LANES]
            o_ref[:, lo:lo + LANES] = (x1 * cos - x2 * sin).astype(BF16)
            o_ref[:, lo + LANES:lo + 2 * LANES] = (x2 * cos + x1 * sin).astype(BF16)

    @pl.when(j == COL_DA_Q)
    def _():
        rope_da(DA_HEAD_DIM ** -0.5 * LOG2_E)

    @pl.when(j == COL_DA_K)
    def _():
        rope_da(1.0)

    @pl.when(j == COL_R_Q)
    def _():
        rope_ret(1.0)

    @pl.when(j == COL_R_K)
    def _():
        rope_ret(RET_QK_DIM ** -0.5)

    @pl.when((j == COL_DA_V) | (j == COL_R_V))
    def _():
        o_ref[...] = project().astype(BF16)

    @pl.when(j == COL_R_G)
    def _():
        acc = project()
        o_ref[...] = (acc * _sigmoid(acc)).astype(BF16)

    @pl.when(j >= COL_G_A)
    def _():
        o_ref[...] = _sigmoid(project()).astype(BF16)


def _inproj(x2d, g, w_bf16, tab, tm, regions):
    m = x2d.shape[0]
    n_tab = tab.shape[0] // tm
    return pl.pallas_call(
        functools.partial(_inproj_kernel, regions=regions),
        grid=(m // tm, len(regions)),
        in_specs=[
            pl.BlockSpec((tm, D_MODEL), lambda i, j: (i, 0)),
            pl.BlockSpec((1, D_MODEL), lambda i, j: (0, 0)),
            pl.BlockSpec((D_MODEL, REGION), lambda i, j: (0, _region_id(j, regions))),
            pl.BlockSpec((tm, 4 * LANES), lambda i, j: (i % n_tab, 0)),
        ],
        out_specs=pl.BlockSpec((tm, REGION), lambda i, j: (i, j)),
        out_shape=jax.ShapeDtypeStruct((m, len(regions) * REGION), BF16),
        scratch_shapes=[pltpu.VMEM((tm, D_MODEL), BF16)],
        compiler_params=pltpu.CompilerParams(
            dimension_semantics=("arbitrary", "arbitrary"),
            vmem_limit_bytes=VMEM_LIMIT),
        name="inproj",
    )(x2d, g, w_bf16, tab)


def _attn_kernel(lq1_ref, lk1_ref, lq2_ref, lk2_ref, subg_ref, q_ref, k_ref, v_ref,
                 km_ref, vm_ref, o_ref, m_scr, l_scr, acc_scr, *, tq):
    qi = pl.program_id(2)

    half = tq // 2

    def update(k_src, v_src, krows, qrows, mask, first):
        vblk = v_src[krows, :]
        for c in range(2):
            lanes = slice(c * DA_HEAD_DIM, (c + 1) * DA_HEAD_DIM)
            s = lax.dot_general(q_ref[qrows, lanes], k_src[krows, lanes], NT_DIMS,
                                preferred_element_type=F32)
            if mask is not None:
                s = jnp.where(mask, s, NEG_BIG)
            chunks = [s[:, t * LANES:(t + 1) * LANES] for t in range(s.shape[1] // LANES)]
            m_cur = jnp.max(functools.reduce(jnp.maximum, chunks), axis=-1, keepdims=True)
            if first:
                m_new = jnp.broadcast_to(m_cur, (s.shape[0], LANES))
            else:
                m_old = m_scr[c, qrows, :]
                m_new = jnp.maximum(m_old, m_cur)
                alpha = jnp.exp2(m_old - m_new)
            ps = [jnp.exp2(ch - m_new) for ch in chunks]
            psum = functools.reduce(jnp.add, ps)
            p = ps[0] if len(ps) == 1 else jnp.concatenate(ps, axis=1)
            pv = jnp.dot(p.astype(BF16), vblk, preferred_element_type=F32)
            if first:
                l_scr[c, qrows, :] = psum
                acc_scr[c, qrows, :] = pv
            else:
                l_scr[c, qrows, :] = alpha * l_scr[c, qrows, :] + psum
                acc_scr[c, qrows, :] = (jnp.concatenate([alpha] * (DA_V_DIM // LANES), axis=1)
                                        * acc_scr[c, qrows, :] + pv)
            m_scr[c, qrows, :] = m_new

    every = slice(None)
    meta_mask = lax.broadcasted_iota(jnp.int32, (tq, META_PAD), 1) < N_META
    update(km_ref, vm_ref, every, every, meta_mask, first=True)

    def body(j, carry):
        update(k_ref, v_ref, pl.ds(pl.multiple_of(j * tq, tq), tq), every, None, first=False)
        return carry

    lax.fori_loop(0, qi, body, 0)

    diag = pl.multiple_of(qi * tq, tq)
    def causal(n_q):
        row = lax.broadcasted_iota(jnp.int32, (n_q, half), 0)
        col = lax.broadcasted_iota(jnp.int32, (n_q, half), 1)
        return col <= row

    update(k_ref, v_ref, pl.ds(diag, half), every, causal(tq), first=False)
    update(k_ref, v_ref, pl.ds(pl.multiple_of(diag + half, half), half), slice(half, tq),
           causal(half), first=False)

    lam = (jnp.exp(jnp.sum(lq1_ref[...] * lk1_ref[...], axis=-1, keepdims=True))
           - jnp.exp(jnp.sum(lq2_ref[...] * lk2_ref[...], axis=-1, keepdims=True))
           + LAMBDA_INIT)
    l1 = jnp.sum(l_scr[0], axis=-1, keepdims=True)
    l2 = jnp.sum(l_scr[1], axis=-1, keepdims=True)
    o = acc_scr[0] / l1 - lam * (acc_scr[1] / l2)
    ms = jnp.mean(o * o, axis=-1, keepdims=True)
    o = o * lax.rsqrt(ms + NORM_EPS) * subg_ref[...] * (1.0 - LAMBDA_INIT)
    o_ref[...] = o.astype(BF16)


def _diff_attention(proj, meta_k, meta_v, lq1, lk1, lq2, lk2, subg, batch, seq, tq):
    nq = seq // tq
    vec = pl.BlockSpec((1, DA_HEAD_DIM), lambda b, h, i: (0, 0))
    per_head = REGION // DA_V_DIM
    return pl.pallas_call(
        functools.partial(_attn_kernel, tq=tq),
        grid=(batch, DA_HEADS, nq),
        in_specs=[
            vec, vec, vec, vec,
            pl.BlockSpec((1, DA_V_DIM), lambda b, h, i: (0, 0)),
            pl.BlockSpec((tq, DA_V_DIM), lambda b, h, i: (b * nq + i, COL_DA_Q * per_head + h)),
            pl.BlockSpec((seq, DA_V_DIM), lambda b, h, i: (b, COL_DA_K * per_head + h)),
            pl.BlockSpec((seq, DA_V_DIM), lambda b, h, i: (b, COL_DA_V * per_head + h)),
            pl.BlockSpec((META_PAD, DA_V_DIM), lambda b, h, i: (0, h)),
            pl.BlockSpec((META_PAD, DA_V_DIM), lambda b, h, i: (0, h)),
        ],
        out_specs=pl.BlockSpec((tq, DA_V_DIM), lambda b, h, i: (b * nq + i, h)),
        out_shape=jax.ShapeDtypeStruct((batch * seq, DA_HEADS * DA_V_DIM), BF16),
        scratch_shapes=[
            pltpu.VMEM((2, tq, LANES), F32),
            pltpu.VMEM((2, tq, LANES), F32),
            pltpu.VMEM((2, tq, DA_V_DIM), F32),
        ],
        compiler_params=pltpu.CompilerParams(
            dimension_semantics=("arbitrary", "arbitrary", "arbitrary"),
            vmem_limit_bytes=VMEM_LIMIT),
        name="diff_attention",
    )(lq1, lk1, lq2, lk2, subg, proj, proj, proj, meta_k, meta_v)


def _ret_kernel(lg_ref, q_ref, k_ref, v_ref, g_ref, km_ref, vm_ref, o_ref,
                state_scr, decay_scr, xi_scr, zeta_scr, *, chunk):
    b = pl.program_id(0)
    c = pl.program_id(1)
    heads = [(h, lg_ref[h], slice(h * RET_V_DIM, (h + 1) * RET_V_DIM))
             for h in range(RET_HEADS)]

    @pl.when((b == 0) & (c == 0))
    def _():
        row = lax.broadcasted_iota(jnp.int32, (chunk, chunk), 0)
        col = lax.broadcasted_iota(jnp.int32, (chunk, chunk), 1)
        rel = (row - col).astype(F32)
        idx = lax.broadcasted_iota(jnp.int32, (chunk, RET_V_DIM), 0).astype(F32)
        for h, lg, _ in heads:
            decay_scr[h] = jnp.where(rel >= 0, jnp.exp(lg * jnp.maximum(rel, 0.0)), 0.0)
            xi_scr[h] = jnp.exp(lg * (idx + 1.0))
            zeta_scr[h] = jnp.exp(lg * (chunk - 1.0 - idx))

    @pl.when(c == 0)
    def _():
        midx = lax.broadcasted_iota(jnp.int32, (META_PAD, RET_V_DIM), 0).astype(F32)
        for h, lg, cols in heads:
            mz = jnp.exp(lg * (N_META - 1.0 - midx))
            vz = (vm_ref[:, cols].astype(F32) * mz).astype(BF16)
            state_scr[h] = lax.dot_general(km_ref[:, cols], vz, TN_DIMS,
                                           preferred_element_type=F32)

    for h, lg, cols in heads:
        q = q_ref[:, cols]
        k = k_ref[:, cols]
        v = v_ref[:, cols]
        s = lax.dot_general(q, k, NT_DIMS, preferred_element_type=F32) * decay_scr[h]
        inner = jnp.dot(s.astype(BF16), v, preferred_element_type=F32)
        state = state_scr[h]
        cross = jnp.dot(q, state.astype(BF16), preferred_element_type=F32) * xi_scr[h]
        o = inner + cross
        vz = (v.astype(F32) * zeta_scr[h]).astype(BF16)
        chunk_decay = jnp.exp(jnp.full((1, 1), lg * chunk, F32))
        state_scr[h] = chunk_decay * state + lax.dot_general(
            k, vz, TN_DIMS, preferred_element_type=F32)
        ms = jnp.mean(o * o, axis=-1, keepdims=True)
        o_ref[:, cols] = (o * lax.rsqrt(ms + NORM_EPS)
                          * g_ref[:, cols].astype(F32)).astype(BF16)


def _retention(log_gamma, proj, meta_k, meta_v, batch, seq, chunk):
    nc = seq // chunk
    tok = lambda col: pl.BlockSpec((chunk, REGION), lambda b, c, lg: (b * nc + c, col))
    meta = pl.BlockSpec((META_PAD, REGION), lambda b, c, lg: (0, 0))
    grid_spec = pltpu.PrefetchScalarGridSpec(
        num_scalar_prefetch=1,
        grid=(batch, nc),
        in_specs=[tok(COL_R_Q), tok(COL_R_K), tok(COL_R_V), tok(COL_R_G), meta, meta],
        out_specs=pl.BlockSpec((chunk, REGION), lambda b, c, lg: (b * nc + c, 0)),
        scratch_shapes=[
            pltpu.VMEM((RET_HEADS, RET_QK_DIM, RET_V_DIM), F32),
            pltpu.VMEM((RET_HEADS, chunk, chunk), F32),
            pltpu.VMEM((RET_HEADS, chunk, RET_V_DIM), F32),
            pltpu.VMEM((RET_HEADS, chunk, RET_V_DIM), F32),
        ],
    )
    return pl.pallas_call(
        functools.partial(_ret_kernel, chunk=chunk),
        grid_spec=grid_spec,
        out_shape=jax.ShapeDtypeStruct((batch * seq, RET_HEADS * RET_V_DIM), BF16),
        compiler_params=pltpu.CompilerParams(
            dimension_semantics=("arbitrary", "arbitrary"),
            vmem_limit_bytes=VMEM_LIMIT),
        name="retention",
    )(log_gamma, proj, proj, proj, proj, meta_k, meta_v)


def _merge_kernel(oa_ref, or_ref, ga0_ref, ga1_ref, gr0_ref, gr1_ref, x_ref,
                  wpa_ref, wpr_ref, wo_ref, g2_ref, h_ref, hn_ref):
    oa = oa_ref[...]
    orr = or_ref[...]
    h = x_ref[...]
    for n, (ga_ref, gr_ref) in enumerate(((ga0_ref, gr0_ref), (ga1_ref, gr1_ref))):
        cols = slice(n * REGION, (n + 1) * REGION)
        ya = jnp.dot(oa, wpa_ref[:, cols], preferred_element_type=F32)
        yr = jnp.dot(orr, wpr_ref[:, cols], preferred_element_type=F32)
        merged = ga_ref[...].astype(F32) * ya + gr_ref[...].astype(F32) * yr
        h = h + jnp.dot(merged.astype(BF16), wo_ref[cols, :], preferred_element_type=F32)
    h_ref[...] = h
    ms = jnp.mean(h * h, axis=-1, keepdims=True)
    hn_ref[...] = (h * lax.rsqrt(ms + NORM_EPS) * g2_ref[...]).astype(BF16)


def _merge(oa, orr, proj, x2d, wpa, wpr, wo, g2, tm):
    m = x2d.shape[0]
    row = lambda width: pl.BlockSpec((tm, width), lambda i: (i, 0))
    gate = lambda col: pl.BlockSpec((tm, REGION), lambda i: (i, col))
    whole = lambda shape: pl.BlockSpec(shape, lambda i: (0, 0), pipeline_mode=pl.Buffered(1))
    return pl.pallas_call(
        _merge_kernel,
        grid=(m // tm,),
        in_specs=[
            row(REGION), row(REGION),
            gate(COL_G_A), gate(COL_G_A + 1), gate(COL_G_R), gate(COL_G_R + 1),
            row(D_MODEL),
            whole((REGION, D_MODEL)), whole((REGION, D_MODEL)), whole((D_MODEL, D_MODEL)),
            whole((1, D_MODEL)),
        ],
        out_specs=[row(D_MODEL), row(D_MODEL)],
        out_shape=[jax.ShapeDtypeStruct((m, D_MODEL), F32),
                   jax.ShapeDtypeStruct((m, D_MODEL), BF16)],
        compiler_params=pltpu.CompilerParams(
            dimension_semantics=("arbitrary",),
            vmem_limit_bytes=VMEM_LIMIT),
        name="merge",
    )(oa, orr, proj, proj, proj, proj, x2d, wpa, wpr, wo, g2)


def _mlp_kernel(hn_ref, wup_ref, wdown_ref, h_ref, gf_ref, o_ref):
    f = pl.program_id(1)
    last = pl.num_programs(1) - 1

    def ffn_chunk():
        u = jnp.dot(hn_ref[...], wup_ref[...], preferred_element_type=F32)
        a = jnp.square(jnp.maximum(u, 0.0)).astype(BF16)
        return jnp.dot(a, wdown_ref[...], preferred_element_type=F32)

    @pl.when(f == 0)
    def _():
        o_ref[...] = h_ref[...] + ffn_chunk()

    @pl.when((f > 0) & (f < last))
    def _():
        o_ref[...] += ffn_chunk()

    @pl.when(f == last)
    def _():
        y = o_ref[...] + ffn_chunk()
        ms = jnp.mean(y * y, axis=-1, keepdims=True)
        o_ref[...] = y * lax.rsqrt(ms + NORM_EPS) * gf_ref[...]


def _mlp(hn, h1, wup, wdown, gf, tm, tf):
    m = hn.shape[0]
    return pl.pallas_call(
        _mlp_kernel,
        grid=(m // tm, D_FF // tf),
        in_specs=[
            pl.BlockSpec((tm, D_MODEL), lambda i, f: (i, 0)),
            pl.BlockSpec((D_MODEL, tf), lambda i, f: (0, f)),
            pl.BlockSpec((tf, D_MODEL), lambda i, f: (f, 0)),
            pl.BlockSpec((tm, D_MODEL), lambda i, f: (i, 0)),
            pl.BlockSpec((1, D_MODEL), lambda i, f: (0, 0)),
        ],
        out_specs=pl.BlockSpec((tm, D_MODEL), lambda i, f: (i, 0)),
        out_shape=jax.ShapeDtypeStruct((m, D_MODEL), F32),
        compiler_params=pltpu.CompilerParams(
            dimension_semantics=("arbitrary", "arbitrary"),
            vmem_limit_bytes=VMEM_LIMIT),
        name="mlp",
    )(hn, wup, wdown, h1, gf)


def _rope_table(pos):
    pos = pos.astype(F32)[:, None]

    def cs(half):
        inv = ROPE_THETA ** (-jnp.arange(half, dtype=F32) / half)
        ang = pos * inv[None, :]
        return jnp.cos(ang), jnp.sin(ang)

    cos_a, sin_a = cs(DA_HEAD_DIM // 2)
    cos_r, sin_r = cs(RET_QK_DIM // 2)
    return jnp.concatenate([cos_a, cos_a, -sin_a, sin_a, cos_r, sin_r], axis=-1)


def kernel(x, meta_tokens, norm1_g, w_in, lam_q1, lam_k1, lam_q2, lam_k2, da_subln_g,
           w_pa, w_pr, w_o, norm2_g, w_up, w_down, normf_g):
    batch, seq, d = x.shape
    x2d = x.reshape(batch * seq, d)
    w_in_b = w_in[0].astype(BF16)
    g1 = norm1_g[0].reshape(1, d)

    tab_real = _rope_table(N_META + jnp.arange(seq))
    tab_meta = _rope_table(jnp.arange(N_META))

    proj = _inproj(x2d, g1, w_in_b, tab_real, tm=1024, regions=tuple(range(N_REGIONS)))
    meta_regions = (COL_DA_K, COL_DA_V, COL_R_K, COL_R_V)
    proj_meta = _inproj(meta_tokens.astype(F32), g1, w_in_b, tab_meta, tm=N_META,
                        regions=meta_regions)
    pad = ((0, META_PAD - N_META), (0, 0))

    def sl(col):
        at = meta_regions.index(col) * REGION
        return jnp.pad(proj_meta[:, at:at + REGION], pad)

    row = lambda a: a[0].reshape(1, -1)
    oa = _diff_attention(proj, sl(COL_DA_K), sl(COL_DA_V), row(lam_q1), row(lam_k1),
                         row(lam_q2), row(lam_k2), row(da_subln_g), batch, seq, tq=1024)

    log_gamma = jnp.log(1.0 - 2.0 ** (-5.0 - jnp.arange(RET_HEADS, dtype=F32)))
    orr = _retention(log_gamma, proj, sl(COL_R_K), sl(COL_R_V), batch, seq, chunk=256)

    h1, hn = _merge(oa, orr, proj, x2d, w_pa[0].astype(BF16), w_pr[0].astype(BF16),
                    w_o[0].astype(BF16), row(norm2_g), tm=512)
    out = _mlp(hn, h1, w_up[0].astype(BF16), w_down[0].astype(BF16),
               normf_g.reshape(1, d), tm=512, tf=1024)
    return out.reshape(batch, seq, d)
```

```python
import functools
import math

import jax
import jax.numpy as jnp
from jax import lax
from jax.experimental import pallas as pl
from jax.experimental.pallas import tpu as pltpu

F32 = jnp.float32
BF16 = jnp.bfloat16

D_MODEL = 2048
N_META = 16
ROPE_THETA = 10000.0
NORM_EPS = 1e-6
DA_HEADS = 4
DA_HEAD_DIM = 128
DA_V_DIM = 2 * DA_HEAD_DIM
RET_HEADS = 4
RET_QK_DIM = 256
RET_V_DIM = 256
D_FF = 4 * D_MODEL
LAMBDA_INIT = 0.8 - 0.6 * math.exp(-0.3 * 0)

REGION = 1024
COL_DA_Q, COL_DA_K, COL_DA_V, COL_R_Q, COL_R_K, COL_R_V, COL_R_G, COL_G_A, COL_G_R = (
    0, 1, 2, 3, 4, 5, 6, 7, 9)
N_REGIONS = 11
IN_WIDTH = N_REGIONS * REGION

LANES = 128
META_PAD = 128
ONES_ROWS = 16
NEG_BIG = -1e30
LOG2_E = math.log2(math.e)
VMEM_LIMIT = 58 * 1024 * 1024

NT_DIMS = (((1,), (1,)), ((), ()))
TN_DIMS = (((0,), (0,)), ((), ()))


def _sigmoid(x):
    return 1.0 / (1.0 + jnp.exp(-x))


def _region_id(j, regions):
    if regions == tuple(range(len(regions))):
        return j
    r = regions[0]
    for idx, region in enumerate(regions[1:], 1):
        r = jnp.where(j >= idx, region, r)
    return r


def _inproj_kernel(x_ref, g_ref, w_ref, tab_ref, o_ref, *rest, regions, emit_vt):
    xn_ref = rest[-1]
    first = pl.program_id(1) == 0
    j = _region_id(pl.program_id(1), regions)

    @pl.when(first)
    def _():
        x = x_ref[...]
        ms = jnp.mean(x * x, axis=-1, keepdims=True)
        xn_ref[...] = (x * lax.rsqrt(ms + NORM_EPS) * g_ref[...]).astype(BF16)

    def project():
        return jnp.dot(xn_ref[...], w_ref[...], preferred_element_type=F32)

    def scaled(t, scale):
        return t if scale == 1.0 else t * scale

    def rope_da(scale):
        acc = project()
        cos = scaled(tab_ref[:, 0:LANES], scale)
        sin = scaled(tab_ref[:, LANES:2 * LANES], scale)
        for c in range(REGION // LANES):
            xc = acc[:, c * LANES:(c + 1) * LANES]
            r = xc * cos + pltpu.roll(xc, LANES // 2, 1) * sin
            o_ref[:, c * LANES:(c + 1) * LANES] = r.astype(BF16)

    def rope_ret(scale):
        acc = project()
        cos = scaled(tab_ref[:, 2 * LANES:3 * LANES], scale)
        sin = scaled(tab_ref[:, 3 * LANES:4 * LANES], scale)
        for h in range(RET_HEADS):
            lo = h * RET_QK_DIM
            x1 = acc[:, lo:lo + LANES]
            x2 = acc[:, lo + LANES:lo + 2 * LANES]
            o_ref[:, lo:lo + LANES] = (x1 * cos - x2 * sin).astype(BF16)
            o_ref[:, lo + LANES:lo + 2 * LANES] = (x2 * cos + x1 * sin).astype(BF16)

    @pl.when(j == COL_DA_Q)
    def _():
        rope_da(DA_HEAD_DIM ** -0.5 * LOG2_E)

    @pl.when(j == COL_DA_K)
    def _():
        rope_da(1.0)

    @pl.when(j == COL_R_Q)
    def _():
        rope_ret(1.0)

    @pl.when(j == COL_R_K)
    def _():
        rope_ret(RET_QK_DIM ** -0.5)

    @pl.when(j == COL_DA_V)
    def _():
        acc = project()
        o_ref[...] = acc.astype(BF16)
        if emit_vt:
            rest[0][0] = acc.T.astype(BF16)

    @pl.when(j == COL_R_V)
    def _():
        o_ref[...] = project().astype(BF16)

    @pl.when(j == COL_R_G)
    def _():
        acc = project()
        o_ref[...] = (acc * _sigmoid(acc)).astype(BF16)

    @pl.when(j >= COL_G_A)
    def _():
        o_ref[...] = _sigmoid(project()).astype(BF16)


def _inproj(x2d, g, w_bf16, tab, tm, regions, emit_vt):
    m = x2d.shape[0]
    n_tab = tab.shape[0] // tm
    out_specs = [pl.BlockSpec((tm, REGION), lambda i, j: (i, j))]
    out_shape = [jax.ShapeDtypeStruct((m, len(regions) * REGION), BF16)]
    if emit_vt:
        out_specs.append(pl.BlockSpec((1, REGION, tm), lambda i, j: (i, 0, 0)))
        out_shape.append(jax.ShapeDtypeStruct((m // tm, REGION, tm), BF16))
    return pl.pallas_call(
        functools.partial(_inproj_kernel, regions=regions, emit_vt=emit_vt),
        grid=(m // tm, len(regions)),
        in_specs=[
            pl.BlockSpec((tm, D_MODEL), lambda i, j: (i, 0)),
            pl.BlockSpec((1, D_MODEL), lambda i, j: (0, 0)),
            pl.BlockSpec((D_MODEL, REGION), lambda i, j: (0, _region_id(j, regions))),
            pl.BlockSpec((tm, 4 * LANES), lambda i, j: (i % n_tab, 0)),
        ],
        out_specs=out_specs,
        out_shape=out_shape,
        scratch_shapes=[pltpu.VMEM((tm, D_MODEL), BF16)],
        compiler_params=pltpu.CompilerParams(
            dimension_semantics=("arbitrary", "arbitrary"),
            vmem_limit_bytes=VMEM_LIMIT),
        name="inproj",
    )(x2d, g, w_bf16, tab)


def _attn_kernel(lq1_ref, lk1_ref, lq2_ref, lk2_ref, subg_ref, q_ref, k_ref, vt_ref,
                 km_ref, vmt_ref, o_ref, m_scr, l_scr, acc_scr, *, tq):
    qi = pl.program_id(2)
    half = tq // 2
    every = slice(None)

    def update(k_blk, vt_blk, queries, mask, first):
        vt = vt_blk[...]
        vt = jnp.concatenate([vt, jnp.ones((ONES_ROWS, vt.shape[1]), BF16)], axis=0)
        scores = []
        for c in range(2):
            lanes = slice(c * DA_HEAD_DIM, (c + 1) * DA_HEAD_DIM)
            scores.append(lax.dot_general(k_blk[:, lanes], q_ref[queries, lanes], NT_DIMS,
                                          preferred_element_type=F32))
        for c, s in enumerate(scores):
            if mask is not None:
                s = jnp.where(mask, s, NEG_BIG)
            m_cur = jnp.max(s, axis=0, keepdims=True)
            if first:
                m_new = m_cur
            else:
                m_old = m_scr[c, :, queries]
                m_new = jnp.maximum(m_old, m_cur)
                alpha = jnp.exp2(m_old - m_new)
            p = jnp.exp2(s - m_new).astype(BF16)
            pv_ext = jnp.dot(vt, p, preferred_element_type=F32)
            pv = pv_ext[0:DA_V_DIM]
            psum = pv_ext[DA_V_DIM:DA_V_DIM + 1]
            if first:
                l_scr[c, :, queries] = psum
                acc_scr[c, :, queries] = pv
            else:
                l_scr[c, :, queries] = alpha * l_scr[c, :, queries] + psum
                acc_scr[c, :, queries] = alpha * acc_scr[c, :, queries] + pv
            m_scr[c, :, queries] = m_new

    meta_mask = lax.broadcasted_iota(jnp.int32, (META_PAD, tq), 0) < N_META
    update(km_ref, vmt_ref, every, meta_mask, first=True)

    def body(j, carry):
        update(k_ref.at[pl.ds(pl.multiple_of(j * tq, tq), tq), :], vt_ref.at[j], every, None,
               first=False)
        return carry

    lax.fori_loop(0, qi, body, 0)

    def causal(n_q):
        key = lax.broadcasted_iota(jnp.int32, (half, n_q), 0)
        query = lax.broadcasted_iota(jnp.int32, (half, n_q), 1)
        return key <= query

    diag = pl.multiple_of(qi * tq, tq)
    update(k_ref.at[pl.ds(diag, half), :], vt_ref.at[qi, :, 0:half], every, causal(tq),
           first=False)
    update(k_ref.at[pl.ds(pl.multiple_of(diag + half, half), half), :],
           vt_ref.at[qi, :, half:tq], slice(half, tq), causal(half), first=False)

    lam = (jnp.exp(jnp.sum(lq1_ref[...] * lk1_ref[...], axis=-1, keepdims=True))
           - jnp.exp(jnp.sum(lq2_ref[...] * lk2_ref[...], axis=-1, keepdims=True))
           + LAMBDA_INIT)
    o_t = acc_scr[0] * (1.0 / l_scr[0]) - acc_scr[1] * (lam / l_scr[1])
    o = o_t.T
    ms = jnp.mean(o * o, axis=-1, keepdims=True)
    o = o * lax.rsqrt(ms + NORM_EPS) * subg_ref[...] * (1.0 - LAMBDA_INIT)
    o_ref[...] = o.astype(BF16)


def _diff_attention(proj, vt, meta_k, meta_vt, lq1, lk1, lq2, lk2, subg, batch, seq):
    tq = vt.shape[2]
    nq = seq // tq
    vec = pl.BlockSpec((1, DA_HEAD_DIM), lambda b, h, i: (0, 0))
    per_head = REGION // DA_V_DIM
    return pl.pallas_call(
        functools.partial(_attn_kernel, tq=tq),
        grid=(batch, DA_HEADS, nq),
        in_specs=[
            vec, vec, vec, vec,
            pl.BlockSpec((1, DA_V_DIM), lambda b, h, i: (0, 0)),
            pl.BlockSpec((tq, DA_V_DIM), lambda b, h, i: (b * nq + i, COL_DA_Q * per_head + h)),
            pl.BlockSpec((seq, DA_V_DIM), lambda b, h, i: (b, COL_DA_K * per_head + h)),
            pl.BlockSpec((nq, DA_V_DIM, tq), lambda b, h, i: (b, h, 0)),
            pl.BlockSpec((META_PAD, DA_V_DIM), lambda b, h, i: (0, h)),
            pl.BlockSpec((DA_V_DIM, META_PAD), lambda b, h, i: (h, 0)),
        ],
        out_specs=pl.BlockSpec((tq, DA_V_DIM), lambda b, h, i: (b * nq + i, h)),
        out_shape=jax.ShapeDtypeStruct((batch * seq, DA_HEADS * DA_V_DIM), BF16),
        scratch_shapes=[
            pltpu.VMEM((2, 1, tq), F32),
            pltpu.VMEM((2, 1, tq), F32),
            pltpu.VMEM((2, DA_V_DIM, tq), F32),
        ],
        compiler_params=pltpu.CompilerParams(
            dimension_semantics=("arbitrary", "arbitrary", "arbitrary"),
            vmem_limit_bytes=VMEM_LIMIT),
        name="diff_attention",
    )(lq1, lk1, lq2, lk2, subg, proj, proj, vt, meta_k, meta_vt)


def _ret_kernel(lg_ref, q_ref, k_ref, v_ref, g_ref, km_ref, vm_ref, o_ref,
                state_scr, decay_scr, xi_scr, zeta_scr, *, chunk):
    b = pl.program_id(0)
    c = pl.program_id(1)
    heads = [(h, lg_ref[h], slice(h * RET_V_DIM, (h + 1) * RET_V_DIM))
             for h in range(RET_HEADS)]

    @pl.when((b == 0) & (c == 0))
    def _():
        row = lax.broadcasted_iota(jnp.int32, (chunk, chunk), 0)
        col = lax.broadcasted_iota(jnp.int32, (chunk, chunk), 1)
        rel = (row - col).astype(F32)
        idx = lax.broadcasted_iota(jnp.int32, (chunk, RET_V_DIM), 0).astype(F32)
        for h, lg, _ in heads:
            decay_scr[h] = jnp.where(rel >= 0, jnp.exp(lg * jnp.maximum(rel, 0.0)), 0.0)
            xi_scr[h] = jnp.exp(lg * (idx + 1.0))
            zeta_scr[h] = jnp.exp(lg * (chunk - 1.0 - idx))

    @pl.when(c == 0)
    def _():
        midx = lax.broadcasted_iota(jnp.int32, (META_PAD, RET_V_DIM), 0).astype(F32)
        for h, lg, cols in heads:
            mz = jnp.exp(lg * (N_META - 1.0 - midx))
            vz = (vm_ref[:, cols].astype(F32) * mz).astype(BF16)
            state_scr[h] = lax.dot_general(km_ref[:, cols], vz, TN_DIMS,
                                           preferred_element_type=F32)

    for h, lg, cols in heads:
        q = q_ref[:, cols]
        k = k_ref[:, cols]
        v = v_ref[:, cols]
        s = lax.dot_general(q, k, NT_DIMS, preferred_element_type=F32) * decay_scr[h]
        inner = jnp.dot(s.astype(BF16), v, preferred_element_type=F32)
        state = state_scr[h]
        cross = jnp.dot(q, state.astype(BF16), preferred_element_type=F32) * xi_scr[h]
        o = inner + cross
        vz = (v.astype(F32) * zeta_scr[h]).astype(BF16)
        chunk_decay = jnp.exp(jnp.full((1, 1), lg * chunk, F32))
        state_scr[h] = chunk_decay * state + lax.dot_general(
            k, vz, TN_DIMS, preferred_element_type=F32)
        ms = jnp.mean(o * o, axis=-1, keepdims=True)
        o_ref[:, cols] = (o * lax.rsqrt(ms + NORM_EPS)
                          * g_ref[:, cols].astype(F32)).astype(BF16)


def _retention(log_gamma, proj, meta_k, meta_v, batch, seq, chunk):
    nc = seq // chunk
    tok = lambda col: pl.BlockSpec((chunk, REGION), lambda b, c, lg: (b * nc + c, col))
    meta = pl.BlockSpec((META_PAD, REGION), lambda b, c, lg: (0, 0))
    grid_spec = pltpu.PrefetchScalarGridSpec(
        num_scalar_prefetch=1,
        grid=(batch, nc),
        in_specs=[tok(COL_R_Q), tok(COL_R_K), tok(COL_R_V), tok(COL_R_G), meta, meta],
        out_specs=pl.BlockSpec((chunk, REGION), lambda b, c, lg: (b * nc + c, 0)),
        scratch_shapes=[
            pltpu.VMEM((RET_HEADS, RET_QK_DIM, RET_V_DIM), F32),
            pltpu.VMEM((RET_HEADS, chunk, chunk), F32),
            pltpu.VMEM((RET_HEADS, chunk, RET_V_DIM), F32),
            pltpu.VMEM((RET_HEADS, chunk, RET_V_DIM), F32),
        ],
    )
    return pl.pallas_call(
        functools.partial(_ret_kernel, chunk=chunk),
        grid_spec=grid_spec,
        out_shape=jax.ShapeDtypeStruct((batch * seq, RET_HEADS * RET_V_DIM), BF16),
        compiler_params=pltpu.CompilerParams(
            dimension_semantics=("arbitrary", "arbitrary"),
            vmem_limit_bytes=VMEM_LIMIT),
        name="retention",
    )(log_gamma, proj, proj, proj, proj, meta_k, meta_v)


def _merge_kernel(oa_ref, or_ref, ga0_ref, ga1_ref, gr0_ref, gr1_ref, x_ref,
                  wpa_ref, wpr_ref, wo_ref, g2_ref, h_ref, hn_ref):
    oa = oa_ref[...]
    orr = or_ref[...]
    h = x_ref[...]
    for n, (ga_ref, gr_ref) in enumerate(((ga0_ref, gr0_ref), (ga1_ref, gr1_ref))):
        cols = slice(n * REGION, (n + 1) * REGION)
        ya = jnp.dot(oa, wpa_ref[:, cols], preferred_element_type=F32)
        yr = jnp.dot(orr, wpr_ref[:, cols], preferred_element_type=F32)
        merged = ga_ref[...].astype(F32) * ya + gr_ref[...].astype(F32) * yr
        h = h + jnp.dot(merged.astype(BF16), wo_ref[cols, :], preferred_element_type=F32)
    h_ref[...] = h
    ms = jnp.mean(h * h, axis=-1, keepdims=True)
    hn_ref[...] = (h * lax.rsqrt(ms + NORM_EPS) * g2_ref[...]).astype(BF16)


def _merge(oa, orr, proj, x2d, wpa, wpr, wo, g2, tm):
    m = x2d.shape[0]
    row = lambda width: pl.BlockSpec((tm, width), lambda i: (i, 0))
    gate = lambda col: pl.BlockSpec((tm, REGION), lambda i: (i, col))
    whole = lambda shape: pl.BlockSpec(shape, lambda i: (0, 0), pipeline_mode=pl.Buffered(1))
    return pl.pallas_call(
        _merge_kernel,
        grid=(m // tm,),
        in_specs=[
            row(REGION), row(REGION),
            gate(COL_G_A), gate(COL_G_A + 1), gate(COL_G_R), gate(COL_G_R + 1),
            row(D_MODEL),
            whole((REGION, D_MODEL)), whole((REGION, D_MODEL)), whole((D_MODEL, D_MODEL)),
            whole((1, D_MODEL)),
        ],
        out_specs=[row(D_MODEL), row(D_MODEL)],
        out_shape=[jax.ShapeDtypeStruct((m, D_MODEL), F32),
                   jax.ShapeDtypeStruct((m, D_MODEL), BF16)],
        compiler_params=pltpu.CompilerParams(
            dimension_semantics=("arbitrary",),
            vmem_limit_bytes=VMEM_LIMIT),
        name="merge",
    )(oa, orr, proj, proj, proj, proj, x2d, wpa, wpr, wo, g2)


def _mlp_kernel(hn_ref, wup_ref, wdown_ref, h_ref, gf_ref, o_ref):
    f = pl.program_id(1)
    last = pl.num_programs(1) - 1

    def ffn_chunk():
        u = jnp.dot(hn_ref[...], wup_ref[...], preferred_element_type=F32)
        a = jnp.square(jnp.maximum(u, 0.0)).astype(BF16)
        return jnp.dot(a, wdown_ref[...], preferred_element_type=F32)

    @pl.when(f == 0)
    def _():
        o_ref[...] = h_ref[...] + ffn_chunk()

    @pl.when((f > 0) & (f < last))
    def _():
        o_ref[...] += ffn_chunk()

    @pl.when(f == last)
    def _():
        y = o_ref[...] + ffn_chunk()
        ms = jnp.mean(y * y, axis=-1, keepdims=True)
        o_ref[...] = y * lax.rsqrt(ms + NORM_EPS) * gf_ref[...]


def _mlp(hn, h1, wup, wdown, gf, tm, tf):
    m = hn.shape[0]
    return pl.pallas_call(
        _mlp_kernel,
        grid=(m // tm, D_FF // tf),
        in_specs=[
            pl.BlockSpec((tm, D_MODEL), lambda i, f: (i, 0)),
            pl.BlockSpec((D_MODEL, tf), lambda i, f: (0, f)),
            pl.BlockSpec((tf, D_MODEL), lambda i, f: (f, 0)),
            pl.BlockSpec((tm, D_MODEL), lambda i, f: (i, 0)),
            pl.BlockSpec((1, D_MODEL), lambda i, f: (0, 0)),
        ],
        out_specs=pl.BlockSpec((tm, D_MODEL), lambda i, f: (i, 0)),
        out_shape=jax.ShapeDtypeStruct((m, D_MODEL), F32),
        compiler_params=pltpu.CompilerParams(
            dimension_semantics=("arbitrary", "arbitrary"),
            vmem_limit_bytes=VMEM_LIMIT),
        name="mlp",
    )(hn, wup, wdown, h1, gf)


def _rope_table(pos):
    pos = pos.astype(F32)[:, None]

    def cs(half):
        inv = ROPE_THETA ** (-jnp.arange(half, dtype=F32) / half)
        ang = pos * inv[None, :]
        return jnp.cos(ang), jnp.sin(ang)

    cos_a, sin_a = cs(DA_HEAD_DIM // 2)
    cos_r, sin_r = cs(RET_QK_DIM // 2)
    return jnp.concatenate([cos_a, cos_a, -sin_a, sin_a, cos_r, sin_r], axis=-1)


def kernel(x, meta_tokens, norm1_g, w_in, lam_q1, lam_k1, lam_q2, lam_k2, da_subln_g,
           w_pa, w_pr, w_o, norm2_g, w_up, w_down, normf_g):
    batch, seq, d = x.shape
    x2d = x.reshape(batch * seq, d)
    w_in_b = w_in[0].astype(BF16)
    g1 = norm1_g[0].reshape(1, d)

    tab_real = _rope_table(N_META + jnp.arange(seq))
    tab_meta = _rope_table(jnp.arange(N_META))

    proj, vt = _inproj(x2d, g1, w_in_b, tab_real, tm=1024, regions=tuple(range(N_REGIONS)),
                       emit_vt=True)
    meta_regions = (COL_DA_K, COL_DA_V, COL_R_K, COL_R_V)
    proj_meta, = _inproj(meta_tokens.astype(F32), g1, w_in_b, tab_meta, tm=N_META,
                         regions=meta_regions, emit_vt=False)
    pad = ((0, META_PAD - N_META), (0, 0))

    def sl(col):
        at = meta_regions.index(col) * REGION
        return jnp.pad(proj_meta[:, at:at + REGION], pad)

    row = lambda a: a[0].reshape(1, -1)
    oa = _diff_attention(proj, vt, sl(COL_DA_K), sl(COL_DA_V).T, row(lam_q1), row(lam_k1),
                         row(lam_q2), row(lam_k2), row(da_subln_g), batch, seq)

    log_gamma = jnp.log(1.0 - 2.0 ** (-5.0 - jnp.arange(RET_HEADS, dtype=F32)))
    orr = _retention(log_gamma, proj, sl(COL_R_K), sl(COL_R_V), batch, seq, chunk=256)

    h1, hn = _merge(oa, orr, proj, x2d, w_pa[0].astype(BF16), w_pr[0].astype(BF16),
                    w_o[0].astype(BF16), row(norm2_g), tm=512)
    out = _mlp(hn, h1, w_up[0].astype(BF16), w_down[0].astype(BF16),
               normf_g.reshape(1, d), tm=512, tf=2048)
    return out.reshape(batch, seq, d)
```

```python
import functools
import math

import jax
import jax.numpy as jnp
from jax import lax
from jax.experimental import pallas as pl
from jax.experimental.pallas import tpu as pltpu

F32 = jnp.float32
BF16 = jnp.bfloat16

D_MODEL = 2048
N_META = 16
ROPE_THETA = 10000.0
NORM_EPS = 1e-6
DA_HEADS = 4
DA_HEAD_DIM = 128
DA_V_DIM = 2 * DA_HEAD_DIM
RET_HEADS = 4
RET_QK_DIM = 256
RET_V_DIM = 256
D_FF = 4 * D_MODEL
LAMBDA_INIT = 0.8 - 0.6 * math.exp(-0.3 * 0)

REGION = 1024
COL_DA_Q, COL_DA_K, COL_DA_V, COL_R_Q, COL_R_K, COL_R_V, COL_R_G, COL_G_A, COL_G_R = (
    0, 1, 2, 3, 4, 5, 6, 7, 9)
N_REGIONS = 11
IN_WIDTH = N_REGIONS * REGION

LANES = 128
META_PAD = 128
ONES_ROWS = 16
CAST_COL_CHUNKS = 8
NEG_BIG = -1e30
LOG2_E = math.log2(math.e)
VMEM_LIMIT = 58 * 1024 * 1024

NT_DIMS = (((1,), (1,)), ((), ()))
TN_DIMS = (((0,), (0,)), ((), ()))


def _sigmoid(x):
    return 1.0 / (1.0 + jnp.exp(-x))


def _region_id(j, regions):
    if regions == tuple(range(len(regions))):
        return j
    r = regions[0]
    for idx, region in enumerate(regions[1:], 1):
        r = jnp.where(j >= idx, region, r)
    return r


def _inproj_kernel(x_ref, g_ref, w_ref, tab_ref, *rest, regions, emit_vt, n_cast):
    cast_in = rest[:n_cast]
    o_ref = rest[n_cast]
    vt_ref = rest[n_cast + 1] if emit_vt else None
    cast_out = rest[len(rest) - 1 - n_cast:len(rest) - 1]
    xn_ref = rest[-1]
    first = pl.program_id(1) == 0
    j = _region_id(pl.program_id(1), regions)

    if n_cast:
        @pl.when(pl.program_id(1) < CAST_COL_CHUNKS)
        def _():
            for src, dst in zip(cast_in, cast_out):
                dst[...] = src[...].astype(BF16)

    @pl.when(first)
    def _():
        x = x_ref[...]
        ms = jnp.mean(x * x, axis=-1, keepdims=True)
        xn_ref[...] = (x * lax.rsqrt(ms + NORM_EPS) * g_ref[...]).astype(BF16)

    def project():
        return jnp.dot(xn_ref[...], w_ref[...], preferred_element_type=F32)

    def scaled(t, scale):
        return t if scale == 1.0 else t * scale

    def rope_da(scale):
        acc = project()
        cos = scaled(tab_ref[:, 0:LANES], scale)
        sin = scaled(tab_ref[:, LANES:2 * LANES], scale)
        for c in range(REGION // LANES):
            xc = acc[:, c * LANES:(c + 1) * LANES]
            r = xc * cos + pltpu.roll(xc, LANES // 2, 1) * sin
            o_ref[:, c * LANES:(c + 1) * LANES] = r.astype(BF16)

    def rope_ret(scale):
        acc = project()
        cos = scaled(tab_ref[:, 2 * LANES:3 * LANES], scale)
        sin = scaled(tab_ref[:, 3 * LANES:4 * LANES], scale)
        for h in range(RET_HEADS):
            lo = h * RET_QK_DIM
            x1 = acc[:, lo:lo + LANES]
            x2 = acc[:, lo + LANES:lo + 2 * LANES]
            o_ref[:, lo:lo + LANES] = (x1 * cos - x2 * sin).astype(BF16)
            o_ref[:, lo + LANES:lo + 2 * LANES] = (x2 * cos + x1 * sin).astype(BF16)

    @pl.when(j == COL_DA_Q)
    def _():
        rope_da(DA_HEAD_DIM ** -0.5 * LOG2_E)

    @pl.when(j == COL_DA_K)
    def _():
        rope_da(1.0)

    @pl.when(j == COL_R_Q)
    def _():
        rope_ret(1.0)

    @pl.when(j == COL_R_K)
    def _():
        rope_ret(RET_QK_DIM ** -0.5)

    @pl.when(j == COL_DA_V)
    def _():
        acc = project()
        o_ref[...] = acc.astype(BF16)
        if emit_vt:
            vt_ref[0] = acc.T.astype(BF16)

    @pl.when(j == COL_R_V)
    def _():
        o_ref[...] = project().astype(BF16)

    @pl.when(j == COL_R_G)
    def _():
        acc = project()
        o_ref[...] = (acc * _sigmoid(acc)).astype(BF16)

    @pl.when(j >= COL_G_A)
    def _():
        o_ref[...] = _sigmoid(project()).astype(BF16)


def _inproj(x2d, g, w_bf16, tab, tm, regions, emit_vt, cast=()):
    m = x2d.shape[0]
    n_rows = m // tm
    n_tab = tab.shape[0] // tm
    out_specs = [pl.BlockSpec((tm, REGION), lambda i, j: (i, j))]
    out_shape = [jax.ShapeDtypeStruct((m, len(regions) * REGION), BF16)]
    if emit_vt:
        out_specs.append(pl.BlockSpec((1, REGION, tm), lambda i, j: (i, 0, 0)))
        out_shape.append(jax.ShapeDtypeStruct((n_rows, REGION, tm), BF16))
    assert not cast or len(regions) >= CAST_COL_CHUNKS
    cast_specs = [
        pl.BlockSpec((a.shape[0] // n_rows, a.shape[1] // CAST_COL_CHUNKS),
                     lambda i, j: (i, jnp.minimum(j, CAST_COL_CHUNKS - 1)))
        for a in cast]
    return pl.pallas_call(
        functools.partial(_inproj_kernel, regions=regions, emit_vt=emit_vt, n_cast=len(cast)),
        grid=(n_rows, len(regions)),
        in_specs=[
            pl.BlockSpec((tm, D_MODEL), lambda i, j: (i, 0)),
            pl.BlockSpec((1, D_MODEL), lambda i, j: (0, 0)),
            pl.BlockSpec((D_MODEL, REGION), lambda i, j: (0, _region_id(j, regions))),
            pl.BlockSpec((tm, 4 * LANES), lambda i, j: (i % n_tab, 0)),
        ] + cast_specs,
        out_specs=out_specs + cast_specs,
        out_shape=out_shape + [jax.ShapeDtypeStruct(a.shape, BF16) for a in cast],
        scratch_shapes=[pltpu.VMEM((tm, D_MODEL), BF16)],
        compiler_params=pltpu.CompilerParams(
            dimension_semantics=("arbitrary", "arbitrary"),
            vmem_limit_bytes=VMEM_LIMIT),
        name="inproj",
    )(x2d, g, w_bf16, tab, *cast)


def _attn_kernel(lq1_ref, lk1_ref, lq2_ref, lk2_ref, subg_ref, q_ref, k_ref, vt_ref,
                 km_ref, vmt_ref, o_ref, m_scr, l_scr, acc_scr, *, tq, n_q_tiles):
    qi = pl.program_id(2)
    half = tq // 2
    every = slice(None)

    def scores(k_blk, queries):
        out = []
        for c in range(2):
            lanes = slice(c * DA_HEAD_DIM, (c + 1) * DA_HEAD_DIM)
            out.append(lax.dot_general(k_blk[:, lanes], q_ref[queries, lanes], NT_DIMS,
                                       preferred_element_type=F32))
        return out

    def absorb(block_scores, vt_blk, queries, mask, first):
        vt = vt_blk[...]
        vt = jnp.concatenate([vt, jnp.ones((ONES_ROWS, vt.shape[1]), BF16)], axis=0)
        for c, s in enumerate(block_scores):
            if mask is not None:
                s = jnp.where(mask, s, NEG_BIG)
            m_cur = jnp.max(s, axis=0, keepdims=True)
            if first:
                m_new = m_cur
            else:
                m_old = m_scr[c, :, queries]
                m_new = jnp.maximum(m_old, m_cur)
                alpha = jnp.exp2(m_old - m_new)
            p = jnp.exp2(s - m_new).astype(BF16)
            pv_ext = jnp.dot(vt, p, preferred_element_type=F32)
            pv = pv_ext[0:DA_V_DIM]
            psum = pv_ext[DA_V_DIM:DA_V_DIM + 1]
            if first:
                l_scr[c, :, queries] = psum
                acc_scr[c, :, queries] = pv
            else:
                l_scr[c, :, queries] = alpha * l_scr[c, :, queries] + psum
                acc_scr[c, :, queries] = alpha * acc_scr[c, :, queries] + pv
            m_scr[c, :, queries] = m_new

    def causal(n_q):
        key = lax.broadcasted_iota(jnp.int32, (half, n_q), 0)
        query = lax.broadcasted_iota(jnp.int32, (half, n_q), 1)
        return key <= query

    def key_blocks(n_full):
        meta_mask = lax.broadcasted_iota(jnp.int32, (META_PAD, tq), 0) < N_META
        blocks = [(km_ref, vmt_ref, every, meta_mask, True)]
        for j in range(n_full):
            blocks.append((k_ref.at[j * tq:(j + 1) * tq, :], vt_ref.at[j], every, None, False))
        diag = n_full * tq
        blocks.append((k_ref.at[diag:diag + half, :], vt_ref.at[n_full, :, 0:half], every,
                       causal(tq), False))
        blocks.append((k_ref.at[diag + half:diag + tq, :], vt_ref.at[n_full, :, half:tq],
                       slice(half, tq), causal(half), False))
        return blocks

    for n_full in range(n_q_tiles):
        @pl.when(qi == n_full)
        def _(n_full=n_full):
            blocks = key_blocks(n_full)
            ahead = scores(blocks[0][0], blocks[0][2])
            for t, (_, vt_blk, queries, mask, first) in enumerate(blocks):
                current = ahead
                if t + 1 < len(blocks):
                    ahead = scores(blocks[t + 1][0], blocks[t + 1][2])
                absorb(current, vt_blk, queries, mask, first)

    lam = (jnp.exp(jnp.sum(lq1_ref[...] * lk1_ref[...], axis=-1, keepdims=True))
           - jnp.exp(jnp.sum(lq2_ref[...] * lk2_ref[...], axis=-1, keepdims=True))
           + LAMBDA_INIT)
    o_t = acc_scr[0] * (1.0 / l_scr[0]) - acc_scr[1] * (lam / l_scr[1])
    o = o_t.T
    ms = jnp.mean(o * o, axis=-1, keepdims=True)
    o = o * lax.rsqrt(ms + NORM_EPS) * subg_ref[...] * (1.0 - LAMBDA_INIT)
    o_ref[...] = o.astype(BF16)


def _diff_attention(proj, vt, meta_k, meta_vt, lq1, lk1, lq2, lk2, subg, batch, seq):
    tq = vt.shape[2]
    nq = seq // tq
    vec = pl.BlockSpec((1, DA_HEAD_DIM), lambda b, h, i: (0, 0))
    per_head = REGION // DA_V_DIM
    return pl.pallas_call(
        functools.partial(_attn_kernel, tq=tq, n_q_tiles=nq),
        grid=(batch, DA_HEADS, nq),
        in_specs=[
            vec, vec, vec, vec,
            pl.BlockSpec((1, DA_V_DIM), lambda b, h, i: (0, 0)),
            pl.BlockSpec((tq, DA_V_DIM), lambda b, h, i: (b * nq + i, COL_DA_Q * per_head + h)),
            pl.BlockSpec((seq, DA_V_DIM), lambda b, h, i: (b, COL_DA_K * per_head + h)),
            pl.BlockSpec((nq, DA_V_DIM, tq), lambda b, h, i: (b, h, 0)),
            pl.BlockSpec((META_PAD, DA_V_DIM), lambda b, h, i: (0, h)),
            pl.BlockSpec((DA_V_DIM, META_PAD), lambda b, h, i: (h, 0)),
        ],
        out_specs=pl.BlockSpec((tq, DA_V_DIM), lambda b, h, i: (b * nq + i, h)),
        out_shape=jax.ShapeDtypeStruct((batch * seq, DA_HEADS * DA_V_DIM), BF16),
        scratch_shapes=[
            pltpu.VMEM((2, 1, tq), F32),
            pltpu.VMEM((2, 1, tq), F32),
            pltpu.VMEM((2, DA_V_DIM, tq), F32),
        ],
        compiler_params=pltpu.CompilerParams(
            dimension_semantics=("arbitrary", "arbitrary", "arbitrary"),
            vmem_limit_bytes=VMEM_LIMIT),
        name="diff_attention",
    )(lq1, lk1, lq2, lk2, subg, proj, proj, vt, meta_k, meta_vt)


def _ret_kernel(lg_ref, q_ref, k_ref, v_ref, g_ref, km_ref, vm_ref, o_ref,
                state_scr, decay_scr, xi_scr, zeta_scr, *, chunk):
    b = pl.program_id(0)
    c = pl.program_id(1)
    heads = [(h, lg_ref[h], slice(h * RET_V_DIM, (h + 1) * RET_V_DIM))
             for h in range(RET_HEADS)]

    @pl.when((b == 0) & (c == 0))
    def _():
        row = lax.broadcasted_iota(jnp.int32, (chunk, chunk), 0)
        col = lax.broadcasted_iota(jnp.int32, (chunk, chunk), 1)
        rel = (row - col).astype(F32)
        idx = lax.broadcasted_iota(jnp.int32, (chunk, RET_V_DIM), 0).astype(F32)
        for h, lg, _ in heads:
            decay_scr[h] = jnp.where(rel >= 0, jnp.exp(lg * jnp.maximum(rel, 0.0)), 0.0)
            xi_scr[h] = jnp.exp(lg * (idx + 1.0))
            zeta_scr[h] = jnp.exp(lg * (chunk - 1.0 - idx))

    @pl.when(c == 0)
    def _():
        midx = lax.broadcasted_iota(jnp.int32, (META_PAD, RET_V_DIM), 0).astype(F32)
        for h, lg, cols in heads:
            mz = jnp.exp(lg * (N_META - 1.0 - midx))
            vz = (vm_ref[:, cols].astype(F32) * mz).astype(BF16)
            state_scr[h] = lax.dot_general(km_ref[:, cols], vz, TN_DIMS,
                                           preferred_element_type=F32)

    for h, lg, cols in heads:
        q = q_ref[:, cols]
        k = k_ref[:, cols]
        v = v_ref[:, cols]
        s = lax.dot_general(q, k, NT_DIMS, preferred_element_type=F32) * decay_scr[h]
        inner = jnp.dot(s.astype(BF16), v, preferred_element_type=F32)
        state = state_scr[h]
        cross = jnp.dot(q, state.astype(BF16), preferred_element_type=F32) * xi_scr[h]
        o = inner + cross
        vz = (v.astype(F32) * zeta_scr[h]).astype(BF16)
        chunk_decay = jnp.exp(jnp.full((1, 1), lg * chunk, F32))
        state_scr[h] = chunk_decay * state + lax.dot_general(
            k, vz, TN_DIMS, preferred_element_type=F32)
        ms = jnp.mean(o * o, axis=-1, keepdims=True)
        o_ref[:, cols] = (o * lax.rsqrt(ms + NORM_EPS)
                          * g_ref[:, cols].astype(F32)).astype(BF16)


def _retention(log_gamma, proj, meta_k, meta_v, batch, seq, chunk):
    nc = seq // chunk
    tok = lambda col: pl.BlockSpec((chunk, REGION), lambda b, c, lg: (b * nc + c, col))
    meta = pl.BlockSpec((META_PAD, REGION), lambda b, c, lg: (0, 0))
    grid_spec = pltpu.PrefetchScalarGridSpec(
        num_scalar_prefetch=1,
        grid=(batch, nc),
        in_specs=[tok(COL_R_Q), tok(COL_R_K), tok(COL_R_V), tok(COL_R_G), meta, meta],
        out_specs=pl.BlockSpec((chunk, REGION), lambda b, c, lg: (b * nc + c, 0)),
        scratch_shapes=[
            pltpu.VMEM((RET_HEADS, RET_QK_DIM, RET_V_DIM), F32),
            pltpu.VMEM((RET_HEADS, chunk, chunk), F32),
            pltpu.VMEM((RET_HEADS, chunk, RET_V_DIM), F32),
            pltpu.VMEM((RET_HEADS, chunk, RET_V_DIM), F32),
        ],
    )
    return pl.pallas_call(
        functools.partial(_ret_kernel, chunk=chunk),
        grid_spec=grid_spec,
        out_shape=jax.ShapeDtypeStruct((batch * seq, RET_HEADS * RET_V_DIM), BF16),
        compiler_params=pltpu.CompilerParams(
            dimension_semantics=("arbitrary", "arbitrary"),
            vmem_limit_bytes=VMEM_LIMIT),
        name="retention",
    )(log_gamma, proj, proj, proj, proj, meta_k, meta_v)


def _merge_kernel(oa_ref, or_ref, ga0_ref, ga1_ref, gr0_ref, gr1_ref, x_ref,
                  wpa_ref, wpr_ref, wo_ref, g2_ref, h_ref, hn_ref):
    oa = oa_ref[...]
    orr = or_ref[...]
    h = x_ref[...]
    for n, (ga_ref, gr_ref) in enumerate(((ga0_ref, gr0_ref), (ga1_ref, gr1_ref))):
        cols = slice(n * REGION, (n + 1) * REGION)
        ya = jnp.dot(oa, wpa_ref[:, cols], preferred_element_type=F32)
        yr = jnp.dot(orr, wpr_ref[:, cols], preferred_element_type=F32)
        merged = ga_ref[...].astype(F32) * ya + gr_ref[...].astype(F32) * yr
        h = h + jnp.dot(merged.astype(BF16), wo_ref[cols, :], preferred_element_type=F32)
    h_ref[...] = h
    ms = jnp.mean(h * h, axis=-1, keepdims=True)
    hn_ref[...] = (h * lax.rsqrt(ms + NORM_EPS) * g2_ref[...]).astype(BF16)


def _merge(oa, orr, proj, x2d, wpa, wpr, wo, g2, tm):
    m = x2d.shape[0]
    row = lambda width: pl.BlockSpec((tm, width), lambda i: (i, 0))
    gate = lambda col: pl.BlockSpec((tm, REGION), lambda i: (i, col))
    whole = lambda shape: pl.BlockSpec(shape, lambda i: (0, 0), pipeline_mode=pl.Buffered(1))
    return pl.pallas_call(
        _merge_kernel,
        grid=(m // tm,),
        in_specs=[
            row(REGION), row(REGION),
            gate(COL_G_A), gate(COL_G_A + 1), gate(COL_G_R), gate(COL_G_R + 1),
            row(D_MODEL),
            whole((REGION, D_MODEL)), whole((REGION, D_MODEL)), whole((D_MODEL, D_MODEL)),
            whole((1, D_MODEL)),
        ],
        out_specs=[row(D_MODEL), row(D_MODEL)],
        out_shape=[jax.ShapeDtypeStruct((m, D_MODEL), F32),
                   jax.ShapeDtypeStruct((m, D_MODEL), BF16)],
        compiler_params=pltpu.CompilerParams(
            dimension_semantics=("arbitrary",),
            vmem_limit_bytes=VMEM_LIMIT),
        name="merge",
    )(oa, orr, proj, proj, proj, proj, x2d, wpa, wpr, wo, g2)


def _mlp_kernel(hn_ref, wup_ref, wdown_ref, h_ref, gf_ref, o_ref):
    f = pl.program_id(1)
    last = pl.num_programs(1) - 1

    def ffn_chunk():
        u = jnp.dot(hn_ref[...], wup_ref[...], preferred_element_type=F32)
        a = jnp.square(jnp.maximum(u, 0.0)).astype(BF16)
        return jnp.dot(a, wdown_ref[...], preferred_element_type=F32)

    @pl.when(f == 0)
    def _():
        o_ref[...] = h_ref[...] + ffn_chunk()

    @pl.when((f > 0) & (f < last))
    def _():
        o_ref[...] += ffn_chunk()

    @pl.when(f == last)
    def _():
        y = o_ref[...] + ffn_chunk()
        ms = jnp.mean(y * y, axis=-1, keepdims=True)
        o_ref[...] = y * lax.rsqrt(ms + NORM_EPS) * gf_ref[...]


def _mlp(hn, h1, wup, wdown, gf, tm, tf):
    m = hn.shape[0]
    return pl.pallas_call(
        _mlp_kernel,
        grid=(m // tm, D_FF // tf),
        in_specs=[
            pl.BlockSpec((tm, D_MODEL), lambda i, f: (i, 0)),
            pl.BlockSpec((D_MODEL, tf), lambda i, f: (0, f)),
            pl.BlockSpec((tf, D_MODEL), lambda i, f: (f, 0)),
            pl.BlockSpec((tm, D_MODEL), lambda i, f: (i, 0)),
            pl.BlockSpec((1, D_MODEL), lambda i, f: (0, 0)),
        ],
        out_specs=pl.BlockSpec((tm, D_MODEL), lambda i, f: (i, 0)),
        out_shape=jax.ShapeDtypeStruct((m, D_MODEL), F32),
        compiler_params=pltpu.CompilerParams(
            dimension_semantics=("arbitrary", "arbitrary"),
            vmem_limit_bytes=VMEM_LIMIT),
        name="mlp",
    )(hn, wup, wdown, h1, gf)


def _rope_table(pos):
    pos = pos.astype(F32)[:, None]

    def cs(half):
        inv = ROPE_THETA ** (-jnp.arange(half, dtype=F32) / half)
        ang = pos * inv[None, :]
        return jnp.cos(ang), jnp.sin(ang)

    cos_a, sin_a = cs(DA_HEAD_DIM // 2)
    cos_r, sin_r = cs(RET_QK_DIM // 2)
    return jnp.concatenate([cos_a, cos_a, -sin_a, sin_a, cos_r, sin_r], axis=-1)


def kernel(x, meta_tokens, norm1_g, w_in, lam_q1, lam_k1, lam_q2, lam_k2, da_subln_g,
           w_pa, w_pr, w_o, norm2_g, w_up, w_down, normf_g):
    batch, seq, d = x.shape
    x2d = x.reshape(batch * seq, d)
    w_in_b = w_in[0].astype(BF16)
    g1 = norm1_g[0].reshape(1, d)

    tab_real = _rope_table(N_META + jnp.arange(seq))
    tab_meta = _rope_table(jnp.arange(N_META))

    proj, vt, w_pa_b, w_pr_b, w_o_b, w_up_b, w_down_b = _inproj(
        x2d, g1, w_in_b, tab_real, tm=1024, regions=tuple(range(N_REGIONS)), emit_vt=True,
        cast=(w_pa[0], w_pr[0], w_o[0], w_up[0], w_down[0]))
    meta_regions = (COL_DA_K, COL_DA_V, COL_R_K, COL_R_V)
    proj_meta, = _inproj(meta_tokens.astype(F32), g1, w_in_b, tab_meta, tm=N_META,
                         regions=meta_regions, emit_vt=False)
    pad = ((0, META_PAD - N_META), (0, 0))

    def sl(col):
        at = meta_regions.index(col) * REGION
        return jnp.pad(proj_meta[:, at:at + REGION], pad)

    row = lambda a: a[0].reshape(1, -1)
    oa = _diff_attention(proj, vt, sl(COL_DA_K), sl(COL_DA_V).T, row(lam_q1), row(lam_k1),
                         row(lam_q2), row(lam_k2), row(da_subln_g), batch, seq)

    log_gamma = jnp.log(1.0 - 2.0 ** (-5.0 - jnp.arange(RET_HEADS, dtype=F32)))
    orr = _retention(log_gamma, proj, sl(COL_R_K), sl(COL_R_V), batch, seq, chunk=256)

    h1, hn = _merge(oa, orr, proj, x2d, w_pa_b, w_pr_b, w_o_b, row(norm2_g), tm=512)
    out = _mlp(hn, h1, w_up_b, w_down_b, normf_g.reshape(1, d), tm=512, tf=2048)
    return out.reshape(batch, seq, d)
```

```python
import functools
import math

import jax
import jax.numpy as jnp
from jax import lax
from jax.experimental import pallas as pl
from jax.experimental.pallas import tpu as pltpu

F32 = jnp.float32
BF16 = jnp.bfloat16

D_MODEL = 2048
N_META = 16
ROPE_THETA = 10000.0
NORM_EPS = 1e-6
DA_HEADS = 4
DA_HEAD_DIM = 128
DA_V_DIM = 2 * DA_HEAD_DIM
RET_HEADS = 4
RET_QK_DIM = 256
RET_V_DIM = 256
D_FF = 4 * D_MODEL
LAMBDA_INIT = 0.8 - 0.6 * math.exp(-0.3 * 0)

REGION = 1024
COL_DA_Q, COL_DA_K, COL_DA_V, COL_R_Q, COL_R_K, COL_R_V, COL_R_G, COL_G_A, COL_G_R = (
    0, 1, 2, 3, 4, 5, 6, 7, 9)
N_REGIONS = 11
IN_WIDTH = N_REGIONS * REGION

LANES = 128
META_PAD = 128
ONES_ROWS = 16
CAST_COL_CHUNKS = 8
NEG_BIG = -1e30
LOG2_E = math.log2(math.e)
VMEM_LIMIT = 58 * 1024 * 1024

NT_DIMS = (((1,), (1,)), ((), ()))
TN_DIMS = (((0,), (0,)), ((), ()))


def _sigmoid(x):
    return 0.5 + 0.5 * jnp.tanh(0.5 * x)


def _region_id(j, regions):
    if regions == tuple(range(len(regions))):
        return j
    r = regions[0]
    for idx, region in enumerate(regions[1:], 1):
        r = jnp.where(j >= idx, region, r)
    return r


def _inproj_kernel(x_ref, g_ref, w_ref, tab_ref, *rest, regions, emit_vt, n_cast):
    cast_in = rest[:n_cast]
    o_ref = rest[n_cast]
    vt_ref = rest[n_cast + 1] if emit_vt else None
    cast_out = rest[len(rest) - 1 - n_cast:len(rest) - 1]
    xn_ref = rest[-1]
    first = pl.program_id(1) == 0
    j = _region_id(pl.program_id(1), regions)

    if n_cast:
        @pl.when(pl.program_id(1) < CAST_COL_CHUNKS)
        def _():
            for src, dst in zip(cast_in, cast_out):
                dst[...] = src[...].astype(BF16)

    @pl.when(first)
    def _():
        x = x_ref[...]
        ms = jnp.mean(x * x, axis=-1, keepdims=True)
        xn_ref[...] = (x * lax.rsqrt(ms + NORM_EPS) * g_ref[...]).astype(BF16)

    def project():
        return jnp.dot(xn_ref[...], w_ref[...], preferred_element_type=F32)

    def scaled(t, scale):
        return t if scale == 1.0 else t * scale

    def rope_da(scale):
        acc = project()
        cos = scaled(tab_ref[:, 0:LANES], scale)
        sin = scaled(tab_ref[:, 2 * LANES:3 * LANES], scale)
        for c in range(REGION // LANES):
            xc = acc[:, c * LANES:(c + 1) * LANES]
            r = xc * cos + pltpu.roll(xc, LANES // 2, 1) * sin
            o_ref[:, c * LANES:(c + 1) * LANES] = r.astype(BF16)

    def rope_ret(scale):
        acc = project()
        cos = scaled(tab_ref[:, LANES:2 * LANES], scale)
        sin = scaled(tab_ref[:, 3 * LANES:4 * LANES], scale)
        for h in range(RET_HEADS):
            lo = h * RET_QK_DIM
            x1 = acc[:, lo:lo + LANES]
            x2 = acc[:, lo + LANES:lo + 2 * LANES]
            o_ref[:, lo:lo + LANES] = (x1 * cos - x2 * sin).astype(BF16)
            o_ref[:, lo + LANES:lo + 2 * LANES] = (x2 * cos + x1 * sin).astype(BF16)

    @pl.when(j == COL_DA_Q)
    def _():
        rope_da(DA_HEAD_DIM ** -0.5 * LOG2_E)

    @pl.when(j == COL_DA_K)
    def _():
        rope_da(1.0)

    @pl.when(j == COL_R_Q)
    def _():
        rope_ret(1.0)

    @pl.when(j == COL_R_K)
    def _():
        rope_ret(RET_QK_DIM ** -0.5)

    @pl.when(j == COL_DA_V)
    def _():
        acc = project()
        o_ref[...] = acc.astype(BF16)
        if emit_vt:
            vt_ref[0] = acc.T.astype(BF16)

    @pl.when(j == COL_R_V)
    def _():
        o_ref[...] = project().astype(BF16)

    @pl.when(j == COL_R_G)
    def _():
        acc = project()
        o_ref[...] = (acc * _sigmoid(acc)).astype(BF16)

    @pl.when(j >= COL_G_A)
    def _():
        o_ref[...] = _sigmoid(project()).astype(BF16)


def _inproj(x2d, g, w_bf16, tab, tm, regions, emit_vt, cast=()):
    m = x2d.shape[0]
    n_rows = m // tm
    n_tab = tab.shape[0] // tm
    out_specs = [pl.BlockSpec((tm, REGION), lambda i, j: (i, j))]
    out_shape = [jax.ShapeDtypeStruct((m, len(regions) * REGION), BF16)]
    if emit_vt:
        out_specs.append(pl.BlockSpec((1, REGION, tm), lambda i, j: (i, 0, 0)))
        out_shape.append(jax.ShapeDtypeStruct((n_rows, REGION, tm), BF16))
    assert not cast or len(regions) >= CAST_COL_CHUNKS
    cast_specs = [
        pl.BlockSpec((a.shape[0] // n_rows, a.shape[1] // CAST_COL_CHUNKS),
                     lambda i, j: (i, jnp.minimum(j, CAST_COL_CHUNKS - 1)))
        for a in cast]
    return pl.pallas_call(
        functools.partial(_inproj_kernel, regions=regions, emit_vt=emit_vt, n_cast=len(cast)),
        grid=(n_rows, len(regions)),
        in_specs=[
            pl.BlockSpec((tm, D_MODEL), lambda i, j: (i, 0)),
            pl.BlockSpec((1, D_MODEL), lambda i, j: (0, 0)),
            pl.BlockSpec((D_MODEL, REGION), lambda i, j: (0, _region_id(j, regions))),
            pl.BlockSpec((tm, 4 * LANES), lambda i, j: (i % n_tab, 0)),
        ] + cast_specs,
        out_specs=out_specs + cast_specs,
        out_shape=out_shape + [jax.ShapeDtypeStruct(a.shape, BF16) for a in cast],
        scratch_shapes=[pltpu.VMEM((tm, D_MODEL), BF16)],
        compiler_params=pltpu.CompilerParams(
            dimension_semantics=("arbitrary", "arbitrary"),
            vmem_limit_bytes=VMEM_LIMIT),
        name="inproj",
    )(x2d, g, w_bf16, tab, *cast)


def _attn_kernel(lq1_ref, lk1_ref, lq2_ref, lk2_ref, subg_ref, q_ref, k_ref, vt_ref,
                 km_ref, vmt_ref, o_ref, m_scr, l_scr, acc_scr, *, tq, n_q_tiles):
    qi = pl.program_id(0)
    half = tq // 2
    every = slice(None)

    def scores(k_blk, queries):
        out = []
        for c in range(2):
            lanes = slice(c * DA_HEAD_DIM, (c + 1) * DA_HEAD_DIM)
            out.append(lax.dot_general(k_blk[:, lanes], q_ref[queries, lanes], NT_DIMS,
                                       preferred_element_type=F32))
        return out

    def absorb(block_scores, vt_blk, queries, mask, first):
        vt = vt_blk[...]
        vt = jnp.concatenate([vt, jnp.ones((ONES_ROWS, vt.shape[1]), BF16)], axis=0)
        for c, s in enumerate(block_scores):
            if mask is not None:
                s = jnp.where(mask, s, NEG_BIG)
            m_cur = jnp.max(s, axis=0, keepdims=True)
            if first:
                m_new = m_cur
            else:
                m_old = m_scr[c, :, queries]
                m_new = jnp.maximum(m_old, m_cur)
                alpha = jnp.exp2(m_old - m_new)
            p = jnp.exp2(s - m_new).astype(BF16)
            pv_ext = jnp.dot(vt, p, preferred_element_type=F32)
            pv = pv_ext[0:DA_V_DIM]
            psum = pv_ext[DA_V_DIM:DA_V_DIM + 1]
            if first:
                l_scr[c, :, queries] = psum
                acc_scr[c, :, queries] = pv
            else:
                l_scr[c, :, queries] = alpha * l_scr[c, :, queries] + psum
                acc_scr[c, :, queries] = alpha * acc_scr[c, :, queries] + pv
            m_scr[c, :, queries] = m_new

    def causal(n_q):
        key = lax.broadcasted_iota(jnp.int32, (half, n_q), 0)
        query = lax.broadcasted_iota(jnp.int32, (half, n_q), 1)
        return key <= query

    def key_blocks(n_full):
        meta_mask = lax.broadcasted_iota(jnp.int32, (META_PAD, tq), 0) < N_META
        blocks = [(km_ref, vmt_ref, every, meta_mask, True)]
        for j in range(n_full):
            blocks.append((k_ref.at[j * tq:(j + 1) * tq, :], vt_ref.at[j], every, None, False))
        diag = n_full * tq
        blocks.append((k_ref.at[diag:diag + half, :], vt_ref.at[n_full, :, 0:half], every,
                       causal(tq), False))
        blocks.append((k_ref.at[diag + half:diag + tq, :], vt_ref.at[n_full, :, half:tq],
                       slice(half, tq), causal(half), False))
        return blocks

    for n_full in range(n_q_tiles):
        @pl.when(qi == n_full)
        def _(n_full=n_full):
            blocks = key_blocks(n_full)
            ahead = scores(blocks[0][0], blocks[0][2])
            for t, (_, vt_blk, queries, mask, first) in enumerate(blocks):
                current = ahead
                if t + 1 < len(blocks):
                    ahead = scores(blocks[t + 1][0], blocks[t + 1][2])
                absorb(current, vt_blk, queries, mask, first)

    lam = (jnp.exp(jnp.sum(lq1_ref[...] * lk1_ref[...], axis=-1, keepdims=True))
           - jnp.exp(jnp.sum(lq2_ref[...] * lk2_ref[...], axis=-1, keepdims=True))
           + LAMBDA_INIT)
    o_t = acc_scr[0] * (1.0 / l_scr[0]) - acc_scr[1] * (lam / l_scr[1])
    o = o_t.T
    ms = jnp.mean(o * o, axis=-1, keepdims=True)
    o = o * lax.rsqrt(ms + NORM_EPS) * subg_ref[...] * (1.0 - LAMBDA_INIT)
    o_ref[...] = o.astype(BF16)


def _diff_attention(proj, vt, meta_k, meta_vt, lq1, lk1, lq2, lk2, subg, batch, seq):
    tq = vt.shape[2]
    nq = seq // tq
    vec = pl.BlockSpec((1, DA_HEAD_DIM), lambda i, b, h: (0, 0))
    per_head = REGION // DA_V_DIM
    return pl.pallas_call(
        functools.partial(_attn_kernel, tq=tq, n_q_tiles=nq),
        grid=(nq, batch, DA_HEADS),
        in_specs=[
            vec, vec, vec, vec,
            pl.BlockSpec((1, DA_V_DIM), lambda i, b, h: (0, 0)),
            pl.BlockSpec((tq, DA_V_DIM), lambda i, b, h: (b * nq + i, COL_DA_Q * per_head + h)),
            pl.BlockSpec((seq, DA_V_DIM), lambda i, b, h: (b, COL_DA_K * per_head + h)),
            pl.BlockSpec((nq, DA_V_DIM, tq), lambda i, b, h: (b, h, 0)),
            pl.BlockSpec((META_PAD, DA_V_DIM), lambda i, b, h: (0, h)),
            pl.BlockSpec((DA_V_DIM, META_PAD), lambda i, b, h: (h, 0)),
        ],
        out_specs=pl.BlockSpec((tq, DA_V_DIM), lambda i, b, h: (b * nq + i, h)),
        out_shape=jax.ShapeDtypeStruct((batch * seq, DA_HEADS * DA_V_DIM), BF16),
        scratch_shapes=[
            pltpu.VMEM((2, 1, tq), F32),
            pltpu.VMEM((2, 1, tq), F32),
            pltpu.VMEM((2, DA_V_DIM, tq), F32),
        ],
        compiler_params=pltpu.CompilerParams(
            dimension_semantics=("arbitrary", "arbitrary", "arbitrary"),
            vmem_limit_bytes=VMEM_LIMIT),
        name="diff_attention",
    )(lq1, lk1, lq2, lk2, subg, proj, proj, vt, meta_k, meta_vt)


def _ret_kernel(lg_ref, q_ref, k_ref, v_ref, g_ref, km_ref, vm_ref, o_ref,
                state_scr, decay_scr, xi_scr, zeta_scr, *, chunk):
    b = pl.program_id(0)
    c = pl.program_id(1)
    heads = [(h, lg_ref[h], slice(h * RET_V_DIM, (h + 1) * RET_V_DIM))
             for h in range(RET_HEADS)]

    @pl.when((b == 0) & (c == 0))
    def _():
        row = lax.broadcasted_iota(jnp.int32, (chunk, chunk), 0)
        col = lax.broadcasted_iota(jnp.int32, (chunk, chunk), 1)
        rel = (row - col).astype(F32)
        idx = lax.broadcasted_iota(jnp.int32, (chunk, RET_V_DIM), 0).astype(F32)
        for h, lg, _ in heads:
            decay_scr[h] = jnp.where(rel >= 0, jnp.exp(lg * jnp.maximum(rel, 0.0)), 0.0)
            xi_scr[h] = jnp.exp(lg * (idx + 1.0))
            zeta_scr[h] = jnp.exp(lg * (chunk - 1.0 - idx))

    @pl.when(c == 0)
    def _():
        midx = lax.broadcasted_iota(jnp.int32, (META_PAD, RET_V_DIM), 0).astype(F32)
        for h, lg, cols in heads:
            mz = jnp.exp(lg * (N_META - 1.0 - midx))
            vz = (vm_ref[:, cols].astype(F32) * mz).astype(BF16)
            state_scr[h] = lax.dot_general(km_ref[:, cols], vz, TN_DIMS,
                                           preferred_element_type=F32)

    for h, lg, cols in heads:
        q = q_ref[:, cols]
        k = k_ref[:, cols]
        v = v_ref[:, cols]
        s = lax.dot_general(q, k, NT_DIMS, preferred_element_type=F32) * decay_scr[h]
        inner = jnp.dot(s.astype(BF16), v, preferred_element_type=F32)
        state = state_scr[h]
        cross = jnp.dot(q, state.astype(BF16), preferred_element_type=F32) * xi_scr[h]
        o = inner + cross
        vz = (v.astype(F32) * zeta_scr[h]).astype(BF16)
        chunk_decay = jnp.exp(jnp.full((1, 1), lg * chunk, F32))
        state_scr[h] = chunk_decay * state + lax.dot_general(
            k, vz, TN_DIMS, preferred_element_type=F32)
        ms = jnp.mean(o * o, axis=-1, keepdims=True)
        o_ref[:, cols] = (o * lax.rsqrt(ms + NORM_EPS)
                          * g_ref[:, cols].astype(F32)).astype(BF16)


def _retention(log_gamma, proj, meta_k, meta_v, batch, seq, chunk):
    nc = seq // chunk
    tok = lambda col: pl.BlockSpec((chunk, REGION), lambda b, c, lg: (b * nc + c, col))
    meta = pl.BlockSpec((META_PAD, REGION), lambda b, c, lg: (0, 0))
    grid_spec = pltpu.PrefetchScalarGridSpec(
        num_scalar_prefetch=1,
        grid=(batch, nc),
        in_specs=[tok(COL_R_Q), tok(COL_R_K), tok(COL_R_V), tok(COL_R_G), meta, meta],
        out_specs=pl.BlockSpec((chunk, REGION), lambda b, c, lg: (b * nc + c, 0)),
        scratch_shapes=[
            pltpu.VMEM((RET_HEADS, RET_QK_DIM, RET_V_DIM), F32),
            pltpu.VMEM((RET_HEADS, chunk, chunk), F32),
            pltpu.VMEM((RET_HEADS, chunk, RET_V_DIM), F32),
            pltpu.VMEM((RET_HEADS, chunk, RET_V_DIM), F32),
        ],
    )
    return pl.pallas_call(
        functools.partial(_ret_kernel, chunk=chunk),
        grid_spec=grid_spec,
        out_shape=jax.ShapeDtypeStruct((batch * seq, RET_HEADS * RET_V_DIM), BF16),
        compiler_params=pltpu.CompilerParams(
            dimension_semantics=("arbitrary", "arbitrary"),
            vmem_limit_bytes=VMEM_LIMIT),
        name="retention",
    )(log_gamma, proj, proj, proj, proj, meta_k, meta_v)


def _merge_kernel(oa_ref, or_ref, ga0_ref, ga1_ref, gr0_ref, gr1_ref, x_ref,
                  wpa_ref, wpr_ref, wo_ref, g2_ref, h_ref, hn_ref):
    oa = oa_ref[...]
    orr = or_ref[...]
    h = x_ref[...]
    for n, (ga_ref, gr_ref) in enumerate(((ga0_ref, gr0_ref), (ga1_ref, gr1_ref))):
        cols = slice(n * REGION, (n + 1) * REGION)
        ya = jnp.dot(oa, wpa_ref[:, cols], preferred_element_type=F32)
        yr = jnp.dot(orr, wpr_ref[:, cols], preferred_element_type=F32)
        merged = ga_ref[...].astype(F32) * ya + gr_ref[...].astype(F32) * yr
        h = h + jnp.dot(merged.astype(BF16), wo_ref[cols, :], preferred_element_type=F32)
    h_ref[...] = h
    ms = jnp.mean(h * h, axis=-1, keepdims=True)
    hn_ref[...] = (h * lax.rsqrt(ms + NORM_EPS) * g2_ref[...]).astype(BF16)


def _merge(oa, orr, proj, x2d, wpa, wpr, wo, g2, tm):
    m = x2d.shape[0]
    row = lambda width: pl.BlockSpec((tm, width), lambda i: (i, 0))
    gate = lambda col: pl.BlockSpec((tm, REGION), lambda i: (i, col))
    whole = lambda shape: pl.BlockSpec(shape, lambda i: (0, 0), pipeline_mode=pl.Buffered(1))
    return pl.pallas_call(
        _merge_kernel,
        grid=(m // tm,),
        in_specs=[
            row(REGION), row(REGION),
            gate(COL_G_A), gate(COL_G_A + 1), gate(COL_G_R), gate(COL_G_R + 1),
            row(D_MODEL),
            whole((REGION, D_MODEL)), whole((REGION, D_MODEL)), whole((D_MODEL, D_MODEL)),
            whole((1, D_MODEL)),
        ],
        out_specs=[row(D_MODEL), row(D_MODEL)],
        out_shape=[jax.ShapeDtypeStruct((m, D_MODEL), F32),
                   jax.ShapeDtypeStruct((m, D_MODEL), BF16)],
        compiler_params=pltpu.CompilerParams(
            dimension_semantics=("arbitrary",),
            vmem_limit_bytes=VMEM_LIMIT),
        name="merge",
    )(oa, orr, proj, proj, proj, proj, x2d, wpa, wpr, wo, g2)


def _mlp_kernel(hn_ref, wup_ref, wdown_ref, h_ref, gf_ref, o_ref):
    f = pl.program_id(1)
    last = pl.num_programs(1) - 1

    def ffn_chunk():
        u = jnp.dot(hn_ref[...], wup_ref[...], preferred_element_type=F32)
        a = jnp.square(jnp.maximum(u, 0.0)).astype(BF16)
        return jnp.dot(a, wdown_ref[...], preferred_element_type=F32)

    @pl.when(f == 0)
    def _():
        o_ref[...] = h_ref[...] + ffn_chunk()

    @pl.when((f > 0) & (f < last))
    def _():
        o_ref[...] += ffn_chunk()

    @pl.when(f == last)
    def _():
        y = o_ref[...] + ffn_chunk()
        ms = jnp.mean(y * y, axis=-1, keepdims=True)
        o_ref[...] = y * lax.rsqrt(ms + NORM_EPS) * gf_ref[...]


def _mlp(hn, h1, wup, wdown, gf, tm, tf):
    m = hn.shape[0]
    return pl.pallas_call(
        _mlp_kernel,
        grid=(m // tm, D_FF // tf),
        in_specs=[
            pl.BlockSpec((tm, D_MODEL), lambda i, f: (i, 0)),
            pl.BlockSpec((D_MODEL, tf), lambda i, f: (0, f)),
            pl.BlockSpec((tf, D_MODEL), lambda i, f: (f, 0)),
            pl.BlockSpec((tm, D_MODEL), lambda i, f: (i, 0)),
            pl.BlockSpec((1, D_MODEL), lambda i, f: (0, 0)),
        ],
        out_specs=pl.BlockSpec((tm, D_MODEL), lambda i, f: (i, 0)),
        out_shape=jax.ShapeDtypeStruct((m, D_MODEL), F32),
        compiler_params=pltpu.CompilerParams(
            dimension_semantics=("arbitrary", "arbitrary"),
            vmem_limit_bytes=VMEM_LIMIT),
        name="mlp",
    )(hn, wup, wdown, h1, gf)


def _rope_table(pos):
    def inv_freq(half):
        return ROPE_THETA ** (-jnp.arange(half, dtype=F32) / half)

    inv_a, inv_r = inv_freq(DA_HEAD_DIM // 2), inv_freq(RET_QK_DIM // 2)
    ang = pos.astype(F32)[:, None] * jnp.concatenate([inv_a, inv_a, inv_r])[None, :]
    sign = jnp.concatenate([-jnp.ones_like(inv_a), jnp.ones_like(inv_a), jnp.ones_like(inv_r)])
    return jnp.concatenate([jnp.cos(ang), jnp.sin(ang) * sign[None, :]], axis=-1)


def kernel(x, meta_tokens, norm1_g, w_in, lam_q1, lam_k1, lam_q2, lam_k2, da_subln_g,
           w_pa, w_pr, w_o, norm2_g, w_up, w_down, normf_g):
    batch, seq, d = x.shape
    x2d = x.reshape(batch * seq, d)
    w_in_b = w_in[0].astype(BF16)
    g1 = norm1_g[0].reshape(1, d)

    tab_real = _rope_table(N_META + jnp.arange(seq))
    tab_meta = _rope_table(jnp.arange(N_META))

    proj, vt, w_pa_b, w_pr_b, w_o_b, w_up_b, w_down_b = _inproj(
        x2d, g1, w_in_b, tab_real, tm=1024, regions=tuple(range(N_REGIONS)), emit_vt=True,
        cast=(w_pa[0], w_pr[0], w_o[0], w_up[0], w_down[0]))
    meta_regions = (COL_DA_K, COL_DA_V, COL_R_K, COL_R_V)
    proj_meta, = _inproj(meta_tokens.astype(F32), g1, w_in_b, tab_meta, tm=N_META,
                         regions=meta_regions, emit_vt=False)
    pad = ((0, META_PAD - N_META), (0, 0))

    def sl(col):
        at = meta_regions.index(col) * REGION
        return jnp.pad(proj_meta[:, at:at + REGION], pad)

    row = lambda a: a[0].reshape(1, -1)
    oa = _diff_attention(proj, vt, sl(COL_DA_K), sl(COL_DA_V).T, row(lam_q1), row(lam_k1),
                         row(lam_q2), row(lam_k2), row(da_subln_g), batch, seq)

    log_gamma = jnp.log(1.0 - 2.0 ** (-5.0 - jnp.arange(RET_HEADS, dtype=F32)))
    orr = _retention(log_gamma, proj, sl(COL_R_K), sl(COL_R_V), batch, seq, chunk=256)

    h1, hn = _merge(oa, orr, proj, x2d, w_pa_b, w_pr_b, w_o_b, row(norm2_g), tm=512)
    out = _mlp(hn, h1, w_up_b, w_down_b, normf_g.reshape(1, d), tm=512, tf=2048)
    return out.reshape(batch, seq, d)
```

```python
import functools
import math

import jax
import jax.numpy as jnp
from jax import lax
from jax.experimental import pallas as pl
from jax.experimental.pallas import tpu as pltpu

F32 = jnp.float32
BF16 = jnp.bfloat16

D_MODEL = 2048
N_META = 16
ROPE_THETA = 10000.0
NORM_EPS = 1e-6
DA_HEADS = 4
DA_HEAD_DIM = 128
DA_V_DIM = 2 * DA_HEAD_DIM
RET_HEADS = 4
RET_QK_DIM = 256
RET_V_DIM = 256
D_FF = 4 * D_MODEL
LAMBDA_INIT = 0.8 - 0.6 * math.exp(-0.3 * 0)

REGION = 1024
COL_DA_Q, COL_DA_K, COL_DA_V, COL_R_Q, COL_R_K, COL_R_V, COL_R_G, COL_G_A, COL_G_R = (
    0, 1, 2, 3, 4, 5, 6, 7, 9)
N_REGIONS = 11
IN_WIDTH = N_REGIONS * REGION

LANES = 128
META_PAD = 128
ONES_ROWS = 16
CAST_COL_CHUNKS = 8
NEG_BIG = -1e30
LOG2_E = math.log2(math.e)
VMEM_LIMIT = 58 * 1024 * 1024

NT_DIMS = (((1,), (1,)), ((), ()))
TN_DIMS = (((0,), (0,)), ((), ()))


def _sigmoid(x):
    return 0.5 + 0.5 * jnp.tanh(0.5 * x)


def _region_id(j, regions):
    if regions == tuple(range(len(regions))):
        return j
    r = regions[0]
    for idx, region in enumerate(regions[1:], 1):
        r = jnp.where(j >= idx, region, r)
    return r


def _inproj_kernel(x_ref, g_ref, w_ref, tab_ref, *rest, regions, emit_vt, n_cast):
    cast_in = rest[:n_cast]
    o_ref = rest[n_cast]
    vt_ref = rest[n_cast + 1] if emit_vt else None
    cast_out = rest[len(rest) - 1 - n_cast:len(rest) - 1]
    xn_ref = rest[-1]
    first = pl.program_id(1) == 0
    j = _region_id(pl.program_id(1), regions)

    if n_cast:
        @pl.when(pl.program_id(1) < CAST_COL_CHUNKS)
        def _():
            for src, dst in zip(cast_in, cast_out):
                dst[...] = src[...].astype(BF16)

    @pl.when(first)
    def _():
        x = x_ref[...]
        ms = jnp.mean(x * x, axis=-1, keepdims=True)
        xn_ref[...] = (x * lax.rsqrt(ms + NORM_EPS) * g_ref[...]).astype(BF16)

    def project():
        return jnp.dot(xn_ref[...], w_ref[...], preferred_element_type=F32)

    def scaled(t, scale):
        return t if scale == 1.0 else t * scale

    def rope_da(scale):
        acc = project()
        cos = scaled(tab_ref[:, 0:LANES], scale)
        sin = scaled(tab_ref[:, 2 * LANES:3 * LANES], scale)
        for c in range(REGION // LANES):
            xc = acc[:, c * LANES:(c + 1) * LANES]
            r = xc * cos + pltpu.roll(xc, LANES // 2, 1) * sin
            o_ref[:, c * LANES:(c + 1) * LANES] = r.astype(BF16)

    def rope_ret(scale):
        acc = project()
        cos = scaled(tab_ref[:, LANES:2 * LANES], scale)
        sin = scaled(tab_ref[:, 3 * LANES:4 * LANES], scale)
        for h in range(RET_HEADS):
            lo = h * RET_QK_DIM
            x1 = acc[:, lo:lo + LANES]
            x2 = acc[:, lo + LANES:lo + 2 * LANES]
            o_ref[:, lo:lo + LANES] = (x1 * cos - x2 * sin).astype(BF16)
            o_ref[:, lo + LANES:lo + 2 * LANES] = (x2 * cos + x1 * sin).astype(BF16)

    @pl.when(j == COL_DA_Q)
    def _():
        rope_da(DA_HEAD_DIM ** -0.5 * LOG2_E)

    @pl.when(j == COL_DA_K)
    def _():
        rope_da(1.0)

    @pl.when(j == COL_R_Q)
    def _():
        rope_ret(1.0)

    @pl.when(j == COL_R_K)
    def _():
        rope_ret(RET_QK_DIM ** -0.5)

    @pl.when(j == COL_DA_V)
    def _():
        acc = project()
        o_ref[...] = acc.astype(BF16)
        if emit_vt:
            vt_ref[0] = acc.T.astype(BF16)

    @pl.when(j == COL_R_V)
    def _():
        o_ref[...] = project().astype(BF16)

    @pl.when(j == COL_R_G)
    def _():
        acc = project()
        o_ref[...] = (acc * _sigmoid(acc)).astype(BF16)

    @pl.when(j >= COL_G_A)
    def _():
        o_ref[...] = _sigmoid(project()).astype(BF16)


def _inproj(x2d, g, w_bf16, tab, tm, regions, emit_vt, cast=()):
    m = x2d.shape[0]
    n_rows = m // tm
    n_tab = tab.shape[0] // tm
    out_specs = [pl.BlockSpec((tm, REGION), lambda i, j: (i, j))]
    out_shape = [jax.ShapeDtypeStruct((m, len(regions) * REGION), BF16)]
    if emit_vt:
        out_specs.append(pl.BlockSpec((1, REGION, tm), lambda i, j: (i, 0, 0)))
        out_shape.append(jax.ShapeDtypeStruct((n_rows, REGION, tm), BF16))
    assert not cast or len(regions) >= CAST_COL_CHUNKS
    cast_specs = [
        pl.BlockSpec((a.shape[0] // n_rows, a.shape[1] // CAST_COL_CHUNKS),
                     lambda i, j: (i, jnp.minimum(j, CAST_COL_CHUNKS - 1)))
        for a in cast]
    return pl.pallas_call(
        functools.partial(_inproj_kernel, regions=regions, emit_vt=emit_vt, n_cast=len(cast)),
        grid=(n_rows, len(regions)),
        in_specs=[
            pl.BlockSpec((tm, D_MODEL), lambda i, j: (i, 0)),
            pl.BlockSpec((1, D_MODEL), lambda i, j: (0, 0)),
            pl.BlockSpec((D_MODEL, REGION), lambda i, j: (0, _region_id(j, regions))),
            pl.BlockSpec((tm, 4 * LANES), lambda i, j: (i % n_tab, 0)),
        ] + cast_specs,
        out_specs=out_specs + cast_specs,
        out_shape=out_shape + [jax.ShapeDtypeStruct(a.shape, BF16) for a in cast],
        scratch_shapes=[pltpu.VMEM((tm, D_MODEL), BF16)],
        compiler_params=pltpu.CompilerParams(
            dimension_semantics=("arbitrary", "arbitrary"),
            vmem_limit_bytes=VMEM_LIMIT),
        name="inproj",
    )(x2d, g, w_bf16, tab, *cast)


def _attn_kernel(lq1_ref, lk1_ref, lq2_ref, lk2_ref, subg_ref, q_ref, k_ref, vt_ref,
                 km_ref, vmt_ref, o_ref, m_scr, l_scr, acc_scr, *, tq, n_q_tiles):
    qi = pl.program_id(2)
    half = tq // 2
    every = slice(None)

    def scores(k_blk, queries):
        out = []
        for c in range(2):
            lanes = slice(c * DA_HEAD_DIM, (c + 1) * DA_HEAD_DIM)
            out.append(lax.dot_general(k_blk[:, lanes], q_ref[queries, lanes], NT_DIMS,
                                       preferred_element_type=F32))
        return out

    def absorb(block_scores, vt_blk, queries, mask, first):
        vt = vt_blk[...]
        vt = jnp.concatenate([vt, jnp.ones((ONES_ROWS, vt.shape[1]), BF16)], axis=0)
        for c, s in enumerate(block_scores):
            if mask is not None:
                s = jnp.where(mask, s, NEG_BIG)
            m_cur = jnp.max(s, axis=0, keepdims=True)
            if first:
                m_new = m_cur
            else:
                m_old = m_scr[c, :, queries]
                m_new = jnp.maximum(m_old, m_cur)
                alpha = jnp.exp2(m_old - m_new)
            p = jnp.exp2(s - m_new).astype(BF16)
            pv_ext = jnp.dot(vt, p, preferred_element_type=F32)
            pv = pv_ext[0:DA_V_DIM]
            psum = pv_ext[DA_V_DIM:DA_V_DIM + 1]
            if first:
                l_scr[c, :, queries] = psum
                acc_scr[c, :, queries] = pv
            else:
                l_scr[c, :, queries] = alpha * l_scr[c, :, queries] + psum
                acc_scr[c, :, queries] = alpha * acc_scr[c, :, queries] + pv
            m_scr[c, :, queries] = m_new

    def causal(n_q):
        key = lax.broadcasted_iota(jnp.int32, (half, n_q), 0)
        query = lax.broadcasted_iota(jnp.int32, (half, n_q), 1)
        return key <= query

    def key_blocks(n_full):
        blocks = [(km_ref, vmt_ref, every, None, True)]
        for j in range(n_full):
            blocks.append((k_ref.at[j * tq:(j + 1) * tq, :], vt_ref.at[j], every, None, False))
        diag = n_full * tq
        blocks.append((k_ref.at[diag:diag + half, :], vt_ref.at[n_full, :, 0:half], every,
                       causal(tq), False))
        blocks.append((k_ref.at[diag + half:diag + tq, :], vt_ref.at[n_full, :, half:tq],
                       slice(half, tq), causal(half), False))
        return blocks

    for n_full in range(n_q_tiles):
        @pl.when(qi == n_full)
        def _(n_full=n_full):
            blocks = key_blocks(n_full)
            ahead = scores(blocks[0][0], blocks[0][2])
            for t, (_, vt_blk, queries, mask, first) in enumerate(blocks):
                current = ahead
                if t + 1 < len(blocks):
                    ahead = scores(blocks[t + 1][0], blocks[t + 1][2])
                absorb(current, vt_blk, queries, mask, first)

    lam = (jnp.exp(jnp.sum(lq1_ref[...] * lk1_ref[...], axis=-1, keepdims=True))
           - jnp.exp(jnp.sum(lq2_ref[...] * lk2_ref[...], axis=-1, keepdims=True))
           + LAMBDA_INIT)
    o_t = acc_scr[0] * (1.0 / l_scr[0]) - acc_scr[1] * (lam / l_scr[1])
    o = o_t.T
    ms = jnp.mean(o * o, axis=-1, keepdims=True)
    o = o * lax.rsqrt(ms + NORM_EPS) * subg_ref[...] * (1.0 - LAMBDA_INIT)
    o_ref[...] = o.astype(BF16)


def _diff_attention(proj, vt, meta_k, meta_vt, lq1, lk1, lq2, lk2, subg, batch, seq):
    tq = vt.shape[2]
    nq = seq // tq
    vec = pl.BlockSpec((1, DA_HEAD_DIM), lambda b, h, i: (0, 0))
    per_head = REGION // DA_V_DIM
    return pl.pallas_call(
        functools.partial(_attn_kernel, tq=tq, n_q_tiles=nq),
        grid=(batch, DA_HEADS, nq),
        in_specs=[
            vec, vec, vec, vec,
            pl.BlockSpec((1, DA_V_DIM), lambda b, h, i: (0, 0)),
            pl.BlockSpec((tq, DA_V_DIM), lambda b, h, i: (b * nq + i, COL_DA_Q * per_head + h)),
            pl.BlockSpec((seq, DA_V_DIM), lambda b, h, i: (b, COL_DA_K * per_head + h)),
            pl.BlockSpec((nq, DA_V_DIM, tq), lambda b, h, i: (b, h, 0)),
            pl.BlockSpec((N_META, DA_V_DIM), lambda b, h, i: (0, h)),
            pl.BlockSpec((DA_V_DIM, N_META), lambda b, h, i: (h, 0)),
        ],
        out_specs=pl.BlockSpec((tq, DA_V_DIM), lambda b, h, i: (b * nq + i, h)),
        out_shape=jax.ShapeDtypeStruct((batch * seq, DA_HEADS * DA_V_DIM), BF16),
        scratch_shapes=[
            pltpu.VMEM((2, 1, tq), F32),
            pltpu.VMEM((2, 1, tq), F32),
            pltpu.VMEM((2, DA_V_DIM, tq), F32),
        ],
        compiler_params=pltpu.CompilerParams(
            dimension_semantics=("arbitrary", "arbitrary", "arbitrary"),
            vmem_limit_bytes=VMEM_LIMIT),
        name="diff_attention",
    )(lq1, lk1, lq2, lk2, subg, proj, proj, vt, meta_k, meta_vt)


def _ret_kernel(lg_ref, q_ref, k_ref, v_ref, g_ref, km_ref, vm_ref, o_ref,
                state_scr, decay_scr, xi_scr, zeta_scr, *, chunk):
    b = pl.program_id(0)
    c = pl.program_id(1)
    heads = [(h, lg_ref[h], slice(h * RET_V_DIM, (h + 1) * RET_V_DIM))
             for h in range(RET_HEADS)]

    @pl.when((b == 0) & (c == 0))
    def _():
        row = lax.broadcasted_iota(jnp.int32, (chunk, chunk), 0)
        col = lax.broadcasted_iota(jnp.int32, (chunk, chunk), 1)
        rel = (row - col).astype(F32)
        idx = lax.broadcasted_iota(jnp.int32, (chunk, RET_V_DIM), 0).astype(F32)
        for h, lg, _ in heads:
            decay_scr[h] = jnp.where(rel >= 0, jnp.exp(lg * jnp.maximum(rel, 0.0)), 0.0)
            xi_scr[h] = jnp.exp(lg * (idx + 1.0))
            zeta_scr[h] = jnp.exp(lg * (chunk - 1.0 - idx))

    @pl.when(c == 0)
    def _():
        midx = lax.broadcasted_iota(jnp.int32, (META_PAD, RET_V_DIM), 0).astype(F32)
        for h, lg, cols in heads:
            mz = jnp.exp(lg * (N_META - 1.0 - midx))
            vz = (vm_ref[:, cols].astype(F32) * mz).astype(BF16)
            state_scr[h] = lax.dot_general(km_ref[:, cols], vz, TN_DIMS,
                                           preferred_element_type=F32)

    for h, lg, cols in heads:
        q = q_ref[:, cols]
        k = k_ref[:, cols]
        v = v_ref[:, cols]
        s = lax.dot_general(q, k, NT_DIMS, preferred_element_type=F32) * decay_scr[h]
        inner = jnp.dot(s.astype(BF16), v, preferred_element_type=F32)
        state = state_scr[h]
        cross = jnp.dot(q, state.astype(BF16), preferred_element_type=F32) * xi_scr[h]
        o = inner + cross
        vz = (v.astype(F32) * zeta_scr[h]).astype(BF16)
        chunk_decay = jnp.exp(jnp.full((1, 1), lg * chunk, F32))
        state_scr[h] = chunk_decay * state + lax.dot_general(
            k, vz, TN_DIMS, preferred_element_type=F32)
        ms = jnp.mean(o * o, axis=-1, keepdims=True)
        o_ref[:, cols] = (o * lax.rsqrt(ms + NORM_EPS)
                          * g_ref[:, cols].astype(F32)).astype(BF16)


def _retention(log_gamma, proj, meta_k, meta_v, batch, seq, chunk):
    nc = seq // chunk
    tok = lambda col: pl.BlockSpec((chunk, REGION), lambda b, c, lg: (b * nc + c, col))
    meta = pl.BlockSpec((META_PAD, REGION), lambda b, c, lg: (0, 0))
    grid_spec = pltpu.PrefetchScalarGridSpec(
        num_scalar_prefetch=1,
        grid=(batch, nc),
        in_specs=[tok(COL_R_Q), tok(COL_R_K), tok(COL_R_V), tok(COL_R_G), meta, meta],
        out_specs=pl.BlockSpec((chunk, REGION), lambda b, c, lg: (b * nc + c, 0)),
        scratch_shapes=[
            pltpu.VMEM((RET_HEADS, RET_QK_DIM, RET_V_DIM), F32),
            pltpu.VMEM((RET_HEADS, chunk, chunk), F32),
            pltpu.VMEM((RET_HEADS, chunk, RET_V_DIM), F32),
            pltpu.VMEM((RET_HEADS, chunk, RET_V_DIM), F32),
        ],
    )
    return pl.pallas_call(
        functools.partial(_ret_kernel, chunk=chunk),
        grid_spec=grid_spec,
        out_shape=jax.ShapeDtypeStruct((batch * seq, RET_HEADS * RET_V_DIM), BF16),
        compiler_params=pltpu.CompilerParams(
            dimension_semantics=("arbitrary", "arbitrary"),
            vmem_limit_bytes=VMEM_LIMIT),
        name="retention",
    )(log_gamma, proj, proj, proj, proj, meta_k, meta_v)


def _merge_kernel(oa_ref, or_ref, ga0_ref, ga1_ref, gr0_ref, gr1_ref, x_ref,
                  wpa_ref, wpr_ref, wo_ref, g2_ref, h_ref, hn_ref):
    oa = oa_ref[...]
    orr = or_ref[...]
    h = x_ref[...]
    for n, (ga_ref, gr_ref) in enumerate(((ga0_ref, gr0_ref), (ga1_ref, gr1_ref))):
        cols = slice(n * REGION, (n + 1) * REGION)
        ya = jnp.dot(oa, wpa_ref[:, cols], preferred_element_type=F32)
        yr = jnp.dot(orr, wpr_ref[:, cols], preferred_element_type=F32)
        merged = ga_ref[...].astype(F32) * ya + gr_ref[...].astype(F32) * yr
        h = h + jnp.dot(merged.astype(BF16), wo_ref[cols, :], preferred_element_type=F32)
    h_ref[...] = h
    ms = jnp.mean(h * h, axis=-1, keepdims=True)
    hn_ref[...] = (h * lax.rsqrt(ms + NORM_EPS) * g2_ref[...]).astype(BF16)


def _merge(oa, orr, proj, x2d, wpa, wpr, wo, g2, tm):
    m = x2d.shape[0]
    row = lambda width: pl.BlockSpec((tm, width), lambda i: (i, 0))
    gate = lambda col: pl.BlockSpec((tm, REGION), lambda i: (i, col))
    whole = lambda shape: pl.BlockSpec(shape, lambda i: (0, 0), pipeline_mode=pl.Buffered(1))
    return pl.pallas_call(
        _merge_kernel,
        grid=(m // tm,),
        in_specs=[
            row(REGION), row(REGION),
            gate(COL_G_A), gate(COL_G_A + 1), gate(COL_G_R), gate(COL_G_R + 1),
            row(D_MODEL),
            whole((REGION, D_MODEL)), whole((REGION, D_MODEL)), whole((D_MODEL, D_MODEL)),
            whole((1, D_MODEL)),
        ],
        out_specs=[row(D_MODEL), row(D_MODEL)],
        out_shape=[jax.ShapeDtypeStruct((m, D_MODEL), F32),
                   jax.ShapeDtypeStruct((m, D_MODEL), BF16)],
        compiler_params=pltpu.CompilerParams(
            dimension_semantics=("arbitrary",),
            vmem_limit_bytes=VMEM_LIMIT),
        name="merge",
    )(oa, orr, proj, proj, proj, proj, x2d, wpa, wpr, wo, g2)


def _mlp_kernel(hn_ref, wup_ref, wdown_ref, h_ref, gf_ref, o_ref):
    f = pl.program_id(1)
    last = pl.num_programs(1) - 1

    def ffn_chunk():
        u = jnp.dot(hn_ref[...], wup_ref[...], preferred_element_type=F32)
        a = jnp.square(jnp.maximum(u, 0.0)).astype(BF16)
        return jnp.dot(a, wdown_ref[...], preferred_element_type=F32)

    @pl.when(f == 0)
    def _():
        o_ref[...] = h_ref[...] + ffn_chunk()

    @pl.when((f > 0) & (f < last))
    def _():
        o_ref[...] += ffn_chunk()

    @pl.when(f == last)
    def _():
        y = o_ref[...] + ffn_chunk()
        ms = jnp.mean(y * y, axis=-1, keepdims=True)
        o_ref[...] = y * lax.rsqrt(ms + NORM_EPS) * gf_ref[...]


def _mlp(hn, h1, wup, wdown, gf, tm, tf):
    m = hn.shape[0]
    return pl.pallas_call(
        _mlp_kernel,
        grid=(m // tm, D_FF // tf),
        in_specs=[
            pl.BlockSpec((tm, D_MODEL), lambda i, f: (i, 0)),
            pl.BlockSpec((D_MODEL, tf), lambda i, f: (0, f)),
            pl.BlockSpec((tf, D_MODEL), lambda i, f: (f, 0)),
            pl.BlockSpec((tm, D_MODEL), lambda i, f: (i, 0)),
            pl.BlockSpec((1, D_MODEL), lambda i, f: (0, 0)),
        ],
        out_specs=pl.BlockSpec((tm, D_MODEL), lambda i, f: (i, 0)),
        out_shape=jax.ShapeDtypeStruct((m, D_MODEL), F32),
        compiler_params=pltpu.CompilerParams(
            dimension_semantics=("arbitrary", "arbitrary"),
            vmem_limit_bytes=VMEM_LIMIT),
        name="mlp",
    )(hn, wup, wdown, h1, gf)


def _rope_table(pos):
    def inv_freq(half):
        return ROPE_THETA ** (-jnp.arange(half, dtype=F32) / half)

    inv_a, inv_r = inv_freq(DA_HEAD_DIM // 2), inv_freq(RET_QK_DIM // 2)
    ang = pos.astype(F32)[:, None] * jnp.concatenate([inv_a, inv_a, inv_r])[None, :]
    sign = jnp.concatenate([-jnp.ones_like(inv_a), jnp.ones_like(inv_a), jnp.ones_like(inv_r)])
    return jnp.concatenate([jnp.cos(ang), jnp.sin(ang) * sign[None, :]], axis=-1)


def kernel(x, meta_tokens, norm1_g, w_in, lam_q1, lam_k1, lam_q2, lam_k2, da_subln_g,
           w_pa, w_pr, w_o, norm2_g, w_up, w_down, normf_g):
    batch, seq, d = x.shape
    x2d = x.reshape(batch * seq, d)
    w_in_b = w_in[0].astype(BF16)
    g1 = norm1_g[0].reshape(1, d)

    tab_real = _rope_table(N_META + jnp.arange(seq))
    tab_meta = _rope_table(jnp.arange(N_META))

    proj, vt, w_pa_b, w_pr_b, w_o_b, w_up_b, w_down_b = _inproj(
        x2d, g1, w_in_b, tab_real, tm=1024, regions=tuple(range(N_REGIONS)), emit_vt=True,
        cast=(w_pa[0], w_pr[0], w_o[0], w_up[0], w_down[0]))
    meta_regions = (COL_DA_K, COL_DA_V, COL_R_K, COL_R_V)
    proj_meta, = _inproj(meta_tokens.astype(F32), g1, w_in_b, tab_meta, tm=N_META,
                         regions=meta_regions, emit_vt=False)

    def meta(col):
        at = meta_regions.index(col) * REGION
        return proj_meta[:, at:at + REGION]

    sl = lambda col: jnp.pad(meta(col), ((0, META_PAD - N_META), (0, 0)))

    row = lambda a: a[0].reshape(1, -1)
    oa = _diff_attention(proj, vt, meta(COL_DA_K), meta(COL_DA_V).T, row(lam_q1), row(lam_k1),
                         row(lam_q2), row(lam_k2), row(da_subln_g), batch, seq)

    log_gamma = jnp.log(1.0 - 2.0 ** (-5.0 - jnp.arange(RET_HEADS, dtype=F32)))
    orr = _retention(log_gamma, proj, sl(COL_R_K), sl(COL_R_V), batch, seq, chunk=512)

    h1, hn = _merge(oa, orr, proj, x2d, w_pa_b, w_pr_b, w_o_b, row(norm2_g), tm=512)
    out = _mlp(hn, h1, w_up_b, w_down_b, normf_g.reshape(1, d), tm=512, tf=2048)
    return out.reshape(batch, seq, d)
```

```python
import functools
import math

import jax
import jax.numpy as jnp
from jax import lax
from jax.experimental import pallas as pl
from jax.experimental.pallas import tpu as pltpu

F32 = jnp.float32
BF16 = jnp.bfloat16

D_MODEL = 2048
N_META = 16
ROPE_THETA = 10000.0
NORM_EPS = 1e-6
DA_HEADS = 4
DA_HEAD_DIM = 128
DA_V_DIM = 2 * DA_HEAD_DIM
RET_HEADS = 4
RET_QK_DIM = 256
RET_V_DIM = 256
D_FF = 4 * D_MODEL
LAMBDA_INIT = 0.8 - 0.6 * math.exp(-0.3 * 0)

REGION = 1024
COL_DA_Q, COL_DA_K, COL_DA_V, COL_R_Q, COL_R_K, COL_R_V, COL_R_G, COL_G_A, COL_G_R = (
    0, 1, 2, 3, 4, 5, 6, 7, 9)
N_REGIONS = 11

LANES = 128
META_PAD = 128
ONES_ROWS = 16
NEG_BIG = -1e30
LOG2_E = math.log2(math.e)
VMEM_LIMIT = 58 * 1024 * 1024

NT_DIMS = (((1,), (1,)), ((), ()))
TN_DIMS = (((0,), (0,)), ((), ()))


def _sigmoid(x):
    return 0.5 + 0.5 * jnp.tanh(0.5 * x)


def _pick(t, values):
    r = values[0]
    for idx, v in enumerate(values[1:], 1):
        r = jnp.where(t >= idx, v, r)
    return r


def _inproj_kernel(x_hbm, g_ref, wa_ref, wb_ref, tab_ref, o_ref, *rest, tiles, emit_vt, tm):
    vt_ref = rest[0] if emit_vt else None
    xbuf, xn_ref, sem = rest[-3:]
    i = pl.program_id(0)
    t = pl.program_id(1)

    def x_copy(row_tile):
        rows = pl.ds(pl.multiple_of(row_tile * tm, tm), tm)
        return pltpu.make_async_copy(x_hbm.at[rows, :], xbuf, sem)

    @pl.when((t == 0) & (i == 0))
    def _():
        x_copy(0).start()

    @pl.when(t == 0)
    def _():
        x_copy(i).wait()
        x = xbuf[...]
        ms = jnp.mean(x * x, axis=-1, keepdims=True)
        xn_ref[...] = (x * lax.rsqrt(ms + NORM_EPS) * g_ref[...]).astype(BF16)

    @pl.when((t == 1) & (i + 1 < pl.num_programs(0)))
    def _():
        x_copy(i + 1).start()

    def scaled(tab, scale):
        return tab if scale == 1.0 else tab * scale

    def rope_da(acc, base, scale):
        cos = scaled(tab_ref[:, 0:LANES], scale)
        sin = scaled(tab_ref[:, 2 * LANES:3 * LANES], scale)
        for c in range(REGION // LANES):
            xc = acc[:, c * LANES:(c + 1) * LANES]
            r = xc * cos + pltpu.roll(xc, LANES // 2, 1) * sin
            o_ref[:, base + c * LANES:base + (c + 1) * LANES] = r.astype(BF16)

    def rope_ret(acc, base, scale):
        cos = scaled(tab_ref[:, LANES:2 * LANES], scale)
        sin = scaled(tab_ref[:, 3 * LANES:4 * LANES], scale)
        for h in range(RET_HEADS):
            lo = h * RET_QK_DIM
            x1 = acc[:, lo:lo + LANES]
            x2 = acc[:, lo + LANES:lo + 2 * LANES]
            o_ref[:, base + lo:base + lo + LANES] = (x1 * cos - x2 * sin).astype(BF16)
            o_ref[:, base + lo + LANES:base + lo + 2 * LANES] = (x2 * cos + x1 * sin).astype(BF16)

    def epilogue(region, acc, base):
        out = slice(base, base + REGION)
        if region == COL_DA_Q:
            rope_da(acc, base, DA_HEAD_DIM ** -0.5 * LOG2_E)
        elif region == COL_DA_K:
            rope_da(acc, base, 1.0)
        elif region == COL_R_Q:
            rope_ret(acc, base, 1.0)
        elif region == COL_R_K:
            rope_ret(acc, base, RET_QK_DIM ** -0.5)
        elif region == COL_DA_V:
            o_ref[:, out] = acc.astype(BF16)
            if emit_vt:
                vt_ref[0] = acc.T.astype(BF16)
        elif region == COL_R_V:
            o_ref[:, out] = acc.astype(BF16)
        elif region == COL_R_G:
            o_ref[:, out] = (acc * _sigmoid(acc)).astype(BF16)
        else:
            o_ref[:, out] = _sigmoid(acc).astype(BF16)

    for idx, tile in enumerate(tiles):
        @pl.when(t == idx)
        def _(tile=tile):
            for half, (region, w_ref) in enumerate(zip(tile, (wa_ref, wb_ref))):
                base = half * REGION
                if region is None:
                    o_ref[:, base:base + REGION] = jnp.zeros((tm, REGION), BF16)
                else:
                    acc = jnp.dot(xn_ref[...], w_ref[...], preferred_element_type=F32)
                    epilogue(region, acc, base)


def _inproj(x2d, g, w_bf16, tab, tm, tiles, emit_vt):
    m = x2d.shape[0]
    n_rows = m // tm
    n_tab = tab.shape[0] // tm
    assert len(tiles) >= 2
    first = tuple(a for a, _ in tiles)
    second = tuple(b if b is not None else tiles[idx - 1][1] for idx, (_, b) in enumerate(tiles))
    out_specs = [pl.BlockSpec((tm, 2 * REGION), lambda i, t: (i, t))]
    out_shape = [jax.ShapeDtypeStruct((m, len(tiles) * 2 * REGION), BF16)]
    if emit_vt:
        out_specs.append(pl.BlockSpec((1, REGION, tm), lambda i, t: (i, 0, 0)))
        out_shape.append(jax.ShapeDtypeStruct((n_rows, REGION, tm), BF16))
    return pl.pallas_call(
        functools.partial(_inproj_kernel, tiles=tiles, emit_vt=emit_vt, tm=tm),
        grid=(n_rows, len(tiles)),
        in_specs=[
            pl.BlockSpec(memory_space=pl.ANY),
            pl.BlockSpec((1, D_MODEL), lambda i, t: (0, 0)),
            pl.BlockSpec((D_MODEL, REGION), lambda i, t: (0, _pick(t, first))),
            pl.BlockSpec((D_MODEL, REGION), lambda i, t: (0, _pick(t, second))),
            pl.BlockSpec((tm, 4 * LANES), lambda i, t: (i % n_tab, 0)),
        ],
        out_specs=out_specs,
        out_shape=out_shape,
        scratch_shapes=[
            pltpu.VMEM((tm, D_MODEL), F32),
            pltpu.VMEM((tm, D_MODEL), BF16),
            pltpu.SemaphoreType.DMA(()),
        ],
        compiler_params=pltpu.CompilerParams(
            dimension_semantics=("arbitrary", "arbitrary"),
            vmem_limit_bytes=VMEM_LIMIT),
        name="inproj",
    )(x2d, g, w_bf16, w_bf16, tab)


def _attn_kernel(lq1_ref, lk1_ref, lq2_ref, lk2_ref, subg_ref, q_ref, k_ref, vt_ref,
                 km_ref, vmt_ref, *rest, tq, n_q_tiles, n_cast):
    cast_in = rest[:n_cast]
    o_ref = rest[n_cast]
    cast_out = rest[n_cast + 1:2 * n_cast + 1]
    m_scr, l_scr, acc_scr = rest[2 * n_cast + 1:]
    qi = pl.program_id(2)
    half = tq // 2
    every = slice(None)

    for src, dst in zip(cast_in, cast_out):
        dst[...] = src[...].astype(BF16)

    def scores(k_blk, queries):
        out = []
        for c in range(2):
            lanes = slice(c * DA_HEAD_DIM, (c + 1) * DA_HEAD_DIM)
            out.append(lax.dot_general(k_blk[:, lanes], q_ref[queries, lanes], NT_DIMS,
                                       preferred_element_type=F32))
        return out

    def absorb(block_scores, vt_blk, queries, mask, first):
        vt = vt_blk[...]
        vt = jnp.concatenate([vt, jnp.ones((ONES_ROWS, vt.shape[1]), BF16)], axis=0)
        for c, s in enumerate(block_scores):
            if mask is not None:
                s = jnp.where(mask, s, NEG_BIG)
            m_cur = jnp.max(s, axis=0, keepdims=True)
            if first:
                m_new = m_cur
            else:
                m_old = m_scr[c, :, queries]
                m_new = jnp.maximum(m_old, m_cur)
                alpha = jnp.exp2(m_old - m_new)
            p = jnp.exp2(s - m_new).astype(BF16)
            pv_ext = jnp.dot(vt, p, preferred_element_type=F32)
            pv = pv_ext[0:DA_V_DIM]
            psum = pv_ext[DA_V_DIM:DA_V_DIM + 1]
            if first:
                l_scr[c, :, queries] = psum
                acc_scr[c, :, queries] = pv
            else:
                l_scr[c, :, queries] = alpha * l_scr[c, :, queries] + psum
                acc_scr[c, :, queries] = alpha * acc_scr[c, :, queries] + pv
            m_scr[c, :, queries] = m_new

    def causal(n_q):
        key = lax.broadcasted_iota(jnp.int32, (half, n_q), 0)
        query = lax.broadcasted_iota(jnp.int32, (half, n_q), 1)
        return key <= query

    def key_blocks(n_full):
        meta_mask = lax.broadcasted_iota(jnp.int32, (META_PAD, tq), 0) < N_META
        blocks = [(km_ref, vmt_ref, every, meta_mask, True)]
        for j in range(n_full):
            blocks.append((k_ref.at[j * tq:(j + 1) * tq, :], vt_ref.at[j], every, None, False))
        diag = n_full * tq
        blocks.append((k_ref.at[diag:diag + half, :], vt_ref.at[n_full, :, 0:half], every,
                       causal(tq), False))
        blocks.append((k_ref.at[diag + half:diag + tq, :], vt_ref.at[n_full, :, half:tq],
                       slice(half, tq), causal(half), False))
        return blocks

    for n_full in range(n_q_tiles):
        @pl.when(qi == n_full)
        def _(n_full=n_full):
            blocks = key_blocks(n_full)
            ahead = scores(blocks[0][0], blocks[0][2])
            for t, (_, vt_blk, queries, mask, first) in enumerate(blocks):
                current = ahead
                if t + 1 < len(blocks):
                    ahead = scores(blocks[t + 1][0], blocks[t + 1][2])
                absorb(current, vt_blk, queries, mask, first)

    lam = (jnp.exp(jnp.sum(lq1_ref[...] * lk1_ref[...], axis=-1, keepdims=True))
           - jnp.exp(jnp.sum(lq2_ref[...] * lk2_ref[...], axis=-1, keepdims=True))
           + LAMBDA_INIT)
    o_t = acc_scr[0] * (1.0 / l_scr[0]) - acc_scr[1] * (lam / l_scr[1])
    o = o_t.T
    ms = jnp.mean(o * o, axis=-1, keepdims=True)
    o = o * lax.rsqrt(ms + NORM_EPS) * subg_ref[...] * (1.0 - LAMBDA_INIT)
    o_ref[...] = o.astype(BF16)


def _diff_attention(proj, vt, meta_k, meta_vt, lq1, lk1, lq2, lk2, subg, batch, seq, cast):
    tq = vt.shape[2]
    nq = seq // tq
    n_steps = batch * DA_HEADS * nq
    vec = pl.BlockSpec((1, DA_HEAD_DIM), lambda b, h, i: (0, 0))
    per_head = REGION // DA_V_DIM
    cast_specs = [pl.BlockSpec((a.shape[0] // n_steps, a.shape[1]),
                               lambda b, h, i: ((b * DA_HEADS + h) * nq + i, 0)) for a in cast]
    return pl.pallas_call(
        functools.partial(_attn_kernel, tq=tq, n_q_tiles=nq, n_cast=len(cast)),
        grid=(batch, DA_HEADS, nq),
        in_specs=[
            vec, vec, vec, vec,
            pl.BlockSpec((1, DA_V_DIM), lambda b, h, i: (0, 0)),
            pl.BlockSpec((tq, DA_V_DIM), lambda b, h, i: (b * nq + i, COL_DA_Q * per_head + h)),
            pl.BlockSpec((seq, DA_V_DIM), lambda b, h, i: (b, COL_DA_K * per_head + h)),
            pl.BlockSpec((nq, DA_V_DIM, tq), lambda b, h, i: (b, h, 0)),
            pl.BlockSpec((META_PAD, DA_V_DIM), lambda b, h, i: (0, h)),
            pl.BlockSpec((DA_V_DIM, META_PAD), lambda b, h, i: (h, 0)),
        ] + cast_specs,
        out_specs=[pl.BlockSpec((tq, DA_V_DIM), lambda b, h, i: (b * nq + i, h))] + cast_specs,
        out_shape=[jax.ShapeDtypeStruct((batch * seq, DA_HEADS * DA_V_DIM), BF16)]
        + [jax.ShapeDtypeStruct(a.shape, BF16) for a in cast],
        scratch_shapes=[
            pltpu.VMEM((2, 1, tq), F32),
            pltpu.VMEM((2, 1, tq), F32),
            pltpu.VMEM((2, DA_V_DIM, tq), F32),
        ],
        compiler_params=pltpu.CompilerParams(
            dimension_semantics=("arbitrary", "arbitrary", "arbitrary"),
            vmem_limit_bytes=VMEM_LIMIT),
        name="diff_attention",
    )(lq1, lk1, lq2, lk2, subg, proj, proj, vt, meta_k, meta_vt, *cast)


def _ret_kernel(lg_ref, q_ref, k_ref, v_ref, g_ref, km_ref, vm_ref, o_ref,
                state_scr, decay_scr, xi_scr, zeta_scr, *, chunk):
    b = pl.program_id(0)
    c = pl.program_id(1)
    heads = [(h, lg_ref[h], slice(h * RET_V_DIM, (h + 1) * RET_V_DIM))
             for h in range(RET_HEADS)]

    @pl.when((b == 0) & (c == 0))
    def _():
        row = lax.broadcasted_iota(jnp.int32, (chunk, chunk), 0)
        col = lax.broadcasted_iota(jnp.int32, (chunk, chunk), 1)
        rel = (row - col).astype(F32)
        idx = lax.broadcasted_iota(jnp.int32, (chunk, RET_V_DIM), 0).astype(F32)
        for h, lg, _ in heads:
            decay_scr[h] = jnp.where(rel >= 0, jnp.exp(lg * jnp.maximum(rel, 0.0)), 0.0)
            xi_scr[h] = jnp.exp(lg * (idx + 1.0))
            zeta_scr[h] = jnp.exp(lg * (chunk - 1.0 - idx))

    @pl.when(c == 0)
    def _():
        midx = lax.broadcasted_iota(jnp.int32, (META_PAD, RET_V_DIM), 0).astype(F32)
        for h, lg, cols in heads:
            mz = jnp.exp(lg * (N_META - 1.0 - midx))
            vz = (vm_ref[:, cols].astype(F32) * mz).astype(BF16)
            state_scr[h] = lax.dot_general(km_ref[:, cols], vz, TN_DIMS,
                                           preferred_element_type=F32)

    for h, lg, cols in heads:
        q = q_ref[:, cols]
        k = k_ref[:, cols]
        v = v_ref[:, cols]
        s = lax.dot_general(q, k, NT_DIMS, preferred_element_type=F32) * decay_scr[h]
        inner = jnp.dot(s.astype(BF16), v, preferred_element_type=F32)
        state = state_scr[h]
        cross = jnp.dot(q, state.astype(BF16), preferred_element_type=F32) * xi_scr[h]
        o = inner + cross
        vz = (v.astype(F32) * zeta_scr[h]).astype(BF16)
        chunk_decay = jnp.exp(jnp.full((1, 1), lg * chunk, F32))
        state_scr[h] = chunk_decay * state + lax.dot_general(
            k, vz, TN_DIMS, preferred_element_type=F32)
        ms = jnp.mean(o * o, axis=-1, keepdims=True)
        o_ref[:, cols] = (o * lax.rsqrt(ms + NORM_EPS)
                          * g_ref[:, cols].astype(F32)).astype(BF16)


def _retention(log_gamma, proj, meta_k, meta_v, batch, seq, chunk):
    nc = seq // chunk
    tok = lambda col: pl.BlockSpec((chunk, REGION), lambda b, c, lg: (b * nc + c, col))
    meta = pl.BlockSpec((META_PAD, REGION), lambda b, c, lg: (0, 0))
    grid_spec = pltpu.PrefetchScalarGridSpec(
        num_scalar_prefetch=1,
        grid=(batch, nc),
        in_specs=[tok(COL_R_Q), tok(COL_R_K), tok(COL_R_V), tok(COL_R_G), meta, meta],
        out_specs=pl.BlockSpec((chunk, REGION), lambda b, c, lg: (b * nc + c, 0)),
        scratch_shapes=[
            pltpu.VMEM((RET_HEADS, RET_QK_DIM, RET_V_DIM), F32),
            pltpu.VMEM((RET_HEADS, chunk, chunk), F32),
            pltpu.VMEM((RET_HEADS, chunk, RET_V_DIM), F32),
            pltpu.VMEM((RET_HEADS, chunk, RET_V_DIM), F32),
        ],
    )
    return pl.pallas_call(
        functools.partial(_ret_kernel, chunk=chunk),
        grid_spec=grid_spec,
        out_shape=jax.ShapeDtypeStruct((batch * seq, RET_HEADS * RET_V_DIM), BF16),
        compiler_params=pltpu.CompilerParams(
            dimension_semantics=("arbitrary", "arbitrary"),
            vmem_limit_bytes=VMEM_LIMIT),
        name="retention",
    )(log_gamma, proj, proj, proj, proj, meta_k, meta_v)


def _merge_kernel(oa_ref, or_ref, ga0_ref, ga1_ref, gr0_ref, gr1_ref, x_ref,
                  wpa_ref, wpr_ref, wo_ref, g2_ref, h_ref, hn_ref):
    oa = oa_ref[...]
    orr = or_ref[...]
    h = x_ref[...]
    for n, (ga_ref, gr_ref) in enumerate(((ga0_ref, gr0_ref), (ga1_ref, gr1_ref))):
        cols = slice(n * REGION, (n + 1) * REGION)
        ya = jnp.dot(oa, wpa_ref[:, cols], preferred_element_type=F32)
        yr = jnp.dot(orr, wpr_ref[:, cols], preferred_element_type=F32)
        merged = ga_ref[...].astype(F32) * ya + gr_ref[...].astype(F32) * yr
        h = h + jnp.dot(merged.astype(BF16), wo_ref[cols, :], preferred_element_type=F32)
    h_ref[...] = h
    ms = jnp.mean(h * h, axis=-1, keepdims=True)
    hn_ref[...] = (h * lax.rsqrt(ms + NORM_EPS) * g2_ref[...]).astype(BF16)


def _merge(oa, orr, proj, x2d, wpa, wpr, wo, g2, tm):
    m = x2d.shape[0]
    row = lambda width: pl.BlockSpec((tm, width), lambda i: (i, 0))
    gate = lambda col: pl.BlockSpec((tm, REGION), lambda i: (i, col))
    whole = lambda shape: pl.BlockSpec(shape, lambda i: (0, 0), pipeline_mode=pl.Buffered(1))
    return pl.pallas_call(
        _merge_kernel,
        grid=(m // tm,),
        in_specs=[
            row(REGION), row(REGION),
            gate(COL_G_A), gate(COL_G_A + 1), gate(COL_G_R), gate(COL_G_R + 1),
            row(D_MODEL),
            whole((REGION, D_MODEL)), whole((REGION, D_MODEL)), whole((D_MODEL, D_MODEL)),
            whole((1, D_MODEL)),
        ],
        out_specs=[row(D_MODEL), row(D_MODEL)],
        out_shape=[jax.ShapeDtypeStruct((m, D_MODEL), F32),
                   jax.ShapeDtypeStruct((m, D_MODEL), BF16)],
        compiler_params=pltpu.CompilerParams(
            dimension_semantics=("arbitrary",),
            vmem_limit_bytes=VMEM_LIMIT),
        name="merge",
    )(oa, orr, proj, proj, proj, proj, x2d, wpa, wpr, wo, g2)


def _mlp_kernel(hn_ref, wup_ref, wdown_ref, h_ref, gf_ref, o_ref):
    f = pl.program_id(1)
    last = pl.num_programs(1) - 1

    def ffn_chunk():
        u = jnp.dot(hn_ref[...], wup_ref[...], preferred_element_type=F32)
        a = jnp.square(jnp.maximum(u, 0.0)).astype(BF16)
        return jnp.dot(a, wdown_ref[...], preferred_element_type=F32)

    @pl.when(f == 0)
    def _():
        o_ref[...] = h_ref[...] + ffn_chunk()

    @pl.when((f > 0) & (f < last))
    def _():
        o_ref[...] += ffn_chunk()

    @pl.when(f == last)
    def _():
        y = o_ref[...] + ffn_chunk()
        ms = jnp.mean(y * y, axis=-1, keepdims=True)
        o_ref[...] = y * lax.rsqrt(ms + NORM_EPS) * gf_ref[...]


def _mlp(hn, h1, wup, wdown, gf, tm, tf):
    m = hn.shape[0]
    return pl.pallas_call(
        _mlp_kernel,
        grid=(m // tm, D_FF // tf),
        in_specs=[
            pl.BlockSpec((tm, D_MODEL), lambda i, f: (i, 0)),
            pl.BlockSpec((D_MODEL, tf), lambda i, f: (0, f)),
            pl.BlockSpec((tf, D_MODEL), lambda i, f: (f, 0)),
            pl.BlockSpec((tm, D_MODEL), lambda i, f: (i, 0)),
            pl.BlockSpec((1, D_MODEL), lambda i, f: (0, 0)),
        ],
        out_specs=pl.BlockSpec((tm, D_MODEL), lambda i, f: (i, 0)),
        out_shape=jax.ShapeDtypeStruct((m, D_MODEL), F32),
        compiler_params=pltpu.CompilerParams(
            dimension_semantics=("arbitrary", "arbitrary"),
            vmem_limit_bytes=VMEM_LIMIT),
        name="mlp",
    )(hn, wup, wdown, h1, gf)


def _rope_table(pos):
    def inv_freq(half):
        return ROPE_THETA ** (-jnp.arange(half, dtype=F32) / half)

    inv_a, inv_r = inv_freq(DA_HEAD_DIM // 2), inv_freq(RET_QK_DIM // 2)
    ang = pos.astype(F32)[:, None] * jnp.concatenate([inv_a, inv_a, inv_r])[None, :]
    sign = jnp.concatenate([-jnp.ones_like(inv_a), jnp.ones_like(inv_a), jnp.ones_like(inv_r)])
    return jnp.concatenate([jnp.cos(ang), jnp.sin(ang) * sign[None, :]], axis=-1)


def kernel(x, meta_tokens, norm1_g, w_in, lam_q1, lam_k1, lam_q2, lam_k2, da_subln_g,
           w_pa, w_pr, w_o, norm2_g, w_up, w_down, normf_g):
    batch, seq, d = x.shape
    x2d = x.reshape(batch * seq, d)
    w_in_b = w_in[0].astype(BF16)
    g1 = norm1_g[0].reshape(1, d)

    tab_real = _rope_table(N_META + jnp.arange(seq))
    tab_meta = _rope_table(jnp.arange(N_META))

    main_tiles = tuple((r, r + 1 if r + 1 < N_REGIONS else None) for r in range(0, N_REGIONS, 2))
    proj, vt = _inproj(x2d, g1, w_in_b, tab_real, tm=1024, tiles=main_tiles, emit_vt=True)
    meta_tiles = ((COL_DA_K, COL_DA_V), (COL_R_K, COL_R_V))
    proj_meta, = _inproj(meta_tokens.astype(F32), g1, w_in_b, tab_meta, tm=N_META,
                         tiles=meta_tiles, emit_vt=False)
    meta_regions = tuple(r for tile in meta_tiles for r in tile)
    pad = ((0, META_PAD - N_META), (0, 0))

    def sl(col):
        at = meta_regions.index(col) * REGION
        return jnp.pad(proj_meta[:, at:at + REGION], pad)

    row = lambda a: a[0].reshape(1, -1)
    oa, w_pa_b, w_pr_b, w_o_b, w_up_b, w_down_b = _diff_attention(
        proj, vt, sl(COL_DA_K), sl(COL_DA_V).T, row(lam_q1), row(lam_k1), row(lam_q2),
        row(lam_k2), row(da_subln_g), batch, seq,
        cast=(w_pa[0], w_pr[0], w_o[0], w_up[0], w_down[0]))

    log_gamma = jnp.log(1.0 - 2.0 ** (-5.0 - jnp.arange(RET_HEADS, dtype=F32)))
    orr = _retention(log_gamma, proj, sl(COL_R_K), sl(COL_R_V), batch, seq, chunk=512)

    h1, hn = _merge(oa, orr, proj, x2d, w_pa_b, w_pr_b, w_o_b, row(norm2_g), tm=512)
    out = _mlp(hn, h1, w_up_b, w_down_b, normf_g.reshape(1, d), tm=512, tf=2048)
    return out.reshape(batch, seq, d)
```

```python
import functools
import math

import jax
import jax.numpy as jnp
from jax import lax
from jax.experimental import pallas as pl
from jax.experimental.pallas import tpu as pltpu

F32 = jnp.float32
BF16 = jnp.bfloat16

D_MODEL = 2048
N_META = 16
ROPE_THETA = 10000.0
NORM_EPS = 1e-6
DA_HEADS = 4
DA_HEAD_DIM = 128
DA_V_DIM = 2 * DA_HEAD_DIM
RET_HEADS = 4
RET_QK_DIM = 256
RET_V_DIM = 256
D_FF = 4 * D_MODEL
LAMBDA_INIT = 0.8 - 0.6 * math.exp(-0.3 * 0)

REGION = 1024
COL_DA_Q, COL_DA_K, COL_DA_V, COL_R_Q, COL_R_K, COL_R_V, COL_R_G, COL_G_A, COL_G_R = (
    0, 1, 2, 3, 4, 5, 6, 7, 9)
N_REGIONS = 11

LANES = 128
META_PAD = 128
ONES_ROWS = 16
NEG_BIG = -1e30
LOG2_E = math.log2(math.e)
VMEM_LIMIT = 58 * 1024 * 1024

NT_DIMS = (((1,), (1,)), ((), ()))
TN_DIMS = (((0,), (0,)), ((), ()))


def _sigmoid(x):
    return 0.5 + 0.5 * jnp.tanh(0.5 * x)


def _pick(t, values):
    r = values[0]
    for idx, v in enumerate(values[1:], 1):
        r = jnp.where(t >= idx, v, r)
    return r


def _inproj_kernel(x_hbm, g_ref, wa_ref, wb_ref, tab_ref, o_ref, *rest, tiles, emit_vt, tm):
    vt_ref = rest[0] if emit_vt else None
    xbuf, xn_ref, sem = rest[-3:]
    i = pl.program_id(0)
    t = pl.program_id(1)

    def x_copy(row_tile):
        rows = pl.ds(pl.multiple_of(row_tile * tm, tm), tm)
        return pltpu.make_async_copy(x_hbm.at[rows, :], xbuf, sem)

    @pl.when((t == 0) & (i == 0))
    def _():
        x_copy(0).start()

    @pl.when(t == 0)
    def _():
        x_copy(i).wait()
        x = xbuf[...]
        ms = jnp.mean(x * x, axis=-1, keepdims=True)
        xn_ref[...] = (x * lax.rsqrt(ms + NORM_EPS) * g_ref[...]).astype(BF16)

    @pl.when((t == 1) & (i + 1 < pl.num_programs(0)))
    def _():
        x_copy(i + 1).start()

    def scaled(tab, scale):
        return tab if scale == 1.0 else tab * scale

    def rope_da(acc, base, scale):
        cos = scaled(tab_ref[:, 0:LANES], scale)
        sin = scaled(tab_ref[:, 2 * LANES:3 * LANES], scale)
        for c in range(REGION // LANES):
            xc = acc[:, c * LANES:(c + 1) * LANES]
            r = xc * cos + pltpu.roll(xc, LANES // 2, 1) * sin
            o_ref[:, base + c * LANES:base + (c + 1) * LANES] = r.astype(BF16)

    def rope_ret(acc, base, scale):
        cos = scaled(tab_ref[:, LANES:2 * LANES], scale)
        sin = scaled(tab_ref[:, 3 * LANES:4 * LANES], scale)
        for h in range(RET_HEADS):
            lo = h * RET_QK_DIM
            x1 = acc[:, lo:lo + LANES]
            x2 = acc[:, lo + LANES:lo + 2 * LANES]
            o_ref[:, base + lo:base + lo + LANES] = (x1 * cos - x2 * sin).astype(BF16)
            o_ref[:, base + lo + LANES:base + lo + 2 * LANES] = (x2 * cos + x1 * sin).astype(BF16)

    def epilogue(region, acc, base):
        out = slice(base, base + REGION)
        if region == COL_DA_Q:
            rope_da(acc, base, DA_HEAD_DIM ** -0.5 * LOG2_E)
        elif region == COL_DA_K:
            rope_da(acc, base, 1.0)
        elif region == COL_R_Q:
            rope_ret(acc, base, 1.0)
        elif region == COL_R_K:
            rope_ret(acc, base, RET_QK_DIM ** -0.5)
        elif region == COL_DA_V:
            o_ref[:, out] = acc.astype(BF16)
            if emit_vt:
                vt_ref[0] = acc.T.astype(BF16)
        elif region == COL_R_V:
            o_ref[:, out] = acc.astype(BF16)
        elif region == COL_R_G:
            o_ref[:, out] = (acc * _sigmoid(acc)).astype(BF16)
        else:
            o_ref[:, out] = _sigmoid(acc).astype(BF16)

    for idx, tile in enumerate(tiles):
        @pl.when(t == idx)
        def _(tile=tile):
            for half, (region, w_ref) in enumerate(zip(tile, (wa_ref, wb_ref))):
                base = half * REGION
                if region is None:
                    o_ref[:, base:base + REGION] = jnp.zeros((tm, REGION), BF16)
                else:
                    acc = jnp.dot(xn_ref[...], w_ref[...], preferred_element_type=F32)
                    epilogue(region, acc, base)


def _inproj(x2d, g, w_bf16, tab, tm, tiles, emit_vt):
    m = x2d.shape[0]
    n_rows = m // tm
    n_tab = tab.shape[0] // tm
    assert len(tiles) >= 2
    first = tuple(a for a, _ in tiles)
    second = tuple(b if b is not None else tiles[idx - 1][1] for idx, (_, b) in enumerate(tiles))
    out_specs = [pl.BlockSpec((tm, 2 * REGION), lambda i, t: (i, t))]
    out_shape = [jax.ShapeDtypeStruct((m, len(tiles) * 2 * REGION), BF16)]
    if emit_vt:
        out_specs.append(pl.BlockSpec((1, REGION, tm), lambda i, t: (i, 0, 0)))
        out_shape.append(jax.ShapeDtypeStruct((n_rows, REGION, tm), BF16))
    return pl.pallas_call(
        functools.partial(_inproj_kernel, tiles=tiles, emit_vt=emit_vt, tm=tm),
        grid=(n_rows, len(tiles)),
        in_specs=[
            pl.BlockSpec(memory_space=pl.ANY),
            pl.BlockSpec((1, D_MODEL), lambda i, t: (0, 0)),
            pl.BlockSpec((D_MODEL, REGION), lambda i, t: (0, _pick(t, first))),
            pl.BlockSpec((D_MODEL, REGION), lambda i, t: (0, _pick(t, second))),
            pl.BlockSpec((tm, 4 * LANES), lambda i, t: (i % n_tab, 0)),
        ],
        out_specs=out_specs,
        out_shape=out_shape,
        scratch_shapes=[
            pltpu.VMEM((tm, D_MODEL), F32),
            pltpu.VMEM((tm, D_MODEL), BF16),
            pltpu.SemaphoreType.DMA(()),
        ],
        compiler_params=pltpu.CompilerParams(
            dimension_semantics=("arbitrary", "arbitrary"),
            vmem_limit_bytes=VMEM_LIMIT),
        name="inproj",
    )(x2d, g, w_bf16, w_bf16, tab)


def _attn_kernel(lq1_ref, lk1_ref, lq2_ref, lk2_ref, subg_ref, q_ref, k_ref, vt_ref,
                 km_ref, vmt_ref, *rest, tq, n_q_tiles, n_cast):
    cast_in = rest[:n_cast]
    o_ref = rest[n_cast]
    cast_out = rest[n_cast + 1:2 * n_cast + 1]
    m_scr, l_scr, acc_scr = rest[2 * n_cast + 1:]
    qi = pl.program_id(2)
    half = tq // 2
    every = slice(None)

    for src, dst in zip(cast_in, cast_out):
        dst[...] = src[...].astype(BF16)

    def scores(k_blk, queries):
        out = []
        for c in range(2):
            lanes = slice(c * DA_HEAD_DIM, (c + 1) * DA_HEAD_DIM)
            out.append(lax.dot_general(k_blk[:, lanes], q_ref[queries, lanes], NT_DIMS,
                                       preferred_element_type=F32))
        return out

    def absorb(block_scores, vt_blk, queries, mask, first, running_max):
        vt = vt_blk[...]
        vt = jnp.concatenate([vt, jnp.ones((ONES_ROWS, vt.shape[1]), BF16)], axis=0)
        for c, s in enumerate(block_scores):
            if mask is not None:
                s = jnp.where(mask, s, NEG_BIG)
            alpha = None
            if first:
                m_ref = jnp.max(s, axis=0, keepdims=True)
                m_scr[c, :, queries] = m_ref
            elif running_max:
                m_old = m_scr[c, :, queries]
                m_ref = jnp.maximum(m_old, jnp.max(s, axis=0, keepdims=True))
                alpha = jnp.exp2(m_old - m_ref)
                m_scr[c, :, queries] = m_ref
            else:
                m_ref = m_scr[c, :, queries]
            p = jnp.exp2(s - m_ref).astype(BF16)
            pv_ext = jnp.dot(vt, p, preferred_element_type=F32)
            pv = pv_ext[0:DA_V_DIM]
            psum = pv_ext[DA_V_DIM:DA_V_DIM + 1]
            if first:
                l_scr[c, :, queries] = psum
                acc_scr[c, :, queries] = pv
            elif alpha is None:
                l_scr[c, :, queries] += psum
                acc_scr[c, :, queries] += pv
            else:
                l_scr[c, :, queries] = alpha * l_scr[c, :, queries] + psum
                acc_scr[c, :, queries] = alpha * acc_scr[c, :, queries] + pv

    def causal(n_q):
        key = lax.broadcasted_iota(jnp.int32, (half, n_q), 0)
        query = lax.broadcasted_iota(jnp.int32, (half, n_q), 1)
        return key <= query

    def key_blocks(n_full):
        meta_mask = lax.broadcasted_iota(jnp.int32, (META_PAD, tq), 0) < N_META
        blocks = [(km_ref, vmt_ref, every, meta_mask, True)]
        for j in range(n_full):
            blocks.append((k_ref.at[j * tq:(j + 1) * tq, :], vt_ref.at[j], every, None, False))
        diag = n_full * tq
        blocks.append((k_ref.at[diag:diag + half, :], vt_ref.at[n_full, :, 0:half], every,
                       causal(tq), False))
        blocks.append((k_ref.at[diag + half:diag + tq, :], vt_ref.at[n_full, :, half:tq],
                       slice(half, tq), causal(half), False))
        return blocks

    def attend(running_max):
        for n_full in range(n_q_tiles):
            @pl.when(qi == n_full)
            def _(n_full=n_full):
                blocks = key_blocks(n_full)
                ahead = scores(blocks[0][0], blocks[0][2])
                for t, (_, vt_blk, queries, mask, first) in enumerate(blocks):
                    current = ahead
                    if t + 1 < len(blocks):
                        ahead = scores(blocks[t + 1][0], blocks[t + 1][2])
                    absorb(current, vt_blk, queries, mask, first, running_max)

    lam = (jnp.exp(jnp.sum(lq1_ref[...] * lk1_ref[...], axis=-1, keepdims=True))
           - jnp.exp(jnp.sum(lq2_ref[...] * lk2_ref[...], axis=-1, keepdims=True))
           + LAMBDA_INIT)

    def finish():
        l1, l2 = l_scr[0], l_scr[1]
        o_t = acc_scr[0] * (1.0 / l1) - acc_scr[1] * (lam / l2)
        o = o_t.T
        ms = jnp.mean(o * o, axis=-1, keepdims=True)
        o = o * lax.rsqrt(ms + NORM_EPS) * subg_ref[...] * (1.0 - LAMBDA_INIT)
        o_ref[...] = o.astype(BF16)
        big = jnp.finfo(F32).max
        finite_l = jnp.min(jnp.where((l1 <= big) & (l2 <= big), 1.0, 0.0))
        finite_o = jnp.min(jnp.where(jnp.abs(o_t) <= big, 1.0, 0.0))
        return jnp.minimum(finite_l, finite_o) > 0.5

    attend(running_max=False)
    all_finite = finish()

    @pl.when(jnp.logical_not(all_finite))
    def _():
        attend(running_max=True)
        finish()


def _diff_attention(proj, vt, meta_k, meta_vt, lq1, lk1, lq2, lk2, subg, batch, seq, cast):
    tq = vt.shape[2]
    nq = seq // tq
    n_steps = batch * DA_HEADS * nq
    vec = pl.BlockSpec((1, DA_HEAD_DIM), lambda b, h, i: (0, 0))
    per_head = REGION // DA_V_DIM
    cast_specs = [pl.BlockSpec((a.shape[0] // n_steps, a.shape[1]),
                               lambda b, h, i: ((b * DA_HEADS + h) * nq + i, 0)) for a in cast]
    return pl.pallas_call(
        functools.partial(_attn_kernel, tq=tq, n_q_tiles=nq, n_cast=len(cast)),
        grid=(batch, DA_HEADS, nq),
        in_specs=[
            vec, vec, vec, vec,
            pl.BlockSpec((1, DA_V_DIM), lambda b, h, i: (0, 0)),
            pl.BlockSpec((tq, DA_V_DIM), lambda b, h, i: (b * nq + i, COL_DA_Q * per_head + h)),
            pl.BlockSpec((seq, DA_V_DIM), lambda b, h, i: (b, COL_DA_K * per_head + h)),
            pl.BlockSpec((nq, DA_V_DIM, tq), lambda b, h, i: (b, h, 0)),
            pl.BlockSpec((META_PAD, DA_V_DIM), lambda b, h, i: (0, h)),
            pl.BlockSpec((DA_V_DIM, META_PAD), lambda b, h, i: (h, 0)),
        ] + cast_specs,
        out_specs=[pl.BlockSpec((tq, DA_V_DIM), lambda b, h, i: (b * nq + i, h))] + cast_specs,
        out_shape=[jax.ShapeDtypeStruct((batch * seq, DA_HEADS * DA_V_DIM), BF16)]
        + [jax.ShapeDtypeStruct(a.shape, BF16) for a in cast],
        scratch_shapes=[
            pltpu.VMEM((2, 1, tq), F32),
            pltpu.VMEM((2, 1, tq), F32),
            pltpu.VMEM((2, DA_V_DIM, tq), F32),
        ],
        compiler_params=pltpu.CompilerParams(
            dimension_semantics=("arbitrary", "arbitrary", "arbitrary"),
            vmem_limit_bytes=VMEM_LIMIT),
        name="diff_attention",
    )(lq1, lk1, lq2, lk2, subg, proj, proj, vt, meta_k, meta_vt, *cast)


def _ret_kernel(lg_ref, q_ref, k_ref, v_ref, g_ref, km_ref, vm_ref, o_ref,
                state_scr, decay_scr, xi_scr, zeta_scr, *, chunk):
    b = pl.program_id(0)
    c = pl.program_id(1)
    heads = [(h, lg_ref[h], slice(h * RET_V_DIM, (h + 1) * RET_V_DIM))
             for h in range(RET_HEADS)]

    @pl.when((b == 0) & (c == 0))
    def _():
        row = lax.broadcasted_iota(jnp.int32, (chunk, chunk), 0)
        col = lax.broadcasted_iota(jnp.int32, (chunk, chunk), 1)
        rel = (row - col).astype(F32)
        idx = lax.broadcasted_iota(jnp.int32, (chunk, RET_V_DIM), 0).astype(F32)
        for h, lg, _ in heads:
            decay_scr[h] = jnp.where(rel >= 0, jnp.exp(lg * jnp.maximum(rel, 0.0)), 0.0)
            xi_scr[h] = jnp.exp(lg * (idx + 1.0))
            zeta_scr[h] = jnp.exp(lg * (chunk - 1.0 - idx))

    @pl.when(c == 0)
    def _():
        midx = lax.broadcasted_iota(jnp.int32, (META_PAD, RET_V_DIM), 0).astype(F32)
        for h, lg, cols in heads:
            mz = jnp.exp(lg * (N_META - 1.0 - midx))
            vz = (vm_ref[:, cols].astype(F32) * mz).astype(BF16)
            state_scr[h] = lax.dot_general(km_ref[:, cols], vz, TN_DIMS,
                                           preferred_element_type=F32)

    for h, lg, cols in heads:
        q = q_ref[:, cols]
        k = k_ref[:, cols]
        v = v_ref[:, cols]
        s = lax.dot_general(q, k, NT_DIMS, preferred_element_type=F32) * decay_scr[h]
        inner = jnp.dot(s.astype(BF16), v, preferred_element_type=F32)
        state = state_scr[h]
        cross = jnp.dot(q, state.astype(BF16), preferred_element_type=F32) * xi_scr[h]
        o = inner + cross
        vz = (v.astype(F32) * zeta_scr[h]).astype(BF16)
        chunk_decay = jnp.exp(jnp.full((1, 1), lg * chunk, F32))
        state_scr[h] = chunk_decay * state + lax.dot_general(
            k, vz, TN_DIMS, preferred_element_type=F32)
        ms = jnp.mean(o * o, axis=-1, keepdims=True)
        o_ref[:, cols] = (o * lax.rsqrt(ms + NORM_EPS)
                          * g_ref[:, cols].astype(F32)).astype(BF16)


def _retention(log_gamma, proj, meta_k, meta_v, batch, seq, chunk):
    nc = seq // chunk
    tok = lambda col: pl.BlockSpec((chunk, REGION), lambda b, c, lg: (b * nc + c, col))
    meta = pl.BlockSpec((META_PAD, REGION), lambda b, c, lg: (0, 0))
    grid_spec = pltpu.PrefetchScalarGridSpec(
        num_scalar_prefetch=1,
        grid=(batch, nc),
        in_specs=[tok(COL_R_Q), tok(COL_R_K), tok(COL_R_V), tok(COL_R_G), meta, meta],
        out_specs=pl.BlockSpec((chunk, REGION), lambda b, c, lg: (b * nc + c, 0)),
        scratch_shapes=[
            pltpu.VMEM((RET_HEADS, RET_QK_DIM, RET_V_DIM), F32),
            pltpu.VMEM((RET_HEADS, chunk, chunk), F32),
            pltpu.VMEM((RET_HEADS, chunk, RET_V_DIM), F32),
            pltpu.VMEM((RET_HEADS, chunk, RET_V_DIM), F32),
        ],
    )
    return pl.pallas_call(
        functools.partial(_ret_kernel, chunk=chunk),
        grid_spec=grid_spec,
        out_shape=jax.ShapeDtypeStruct((batch * seq, RET_HEADS * RET_V_DIM), BF16),
        compiler_params=pltpu.CompilerParams(
            dimension_semantics=("arbitrary", "arbitrary"),
            vmem_limit_bytes=VMEM_LIMIT),
        name="retention",
    )(log_gamma, proj, proj, proj, proj, meta_k, meta_v)


def _merge_kernel(oa_ref, or_ref, ga0_ref, ga1_ref, gr0_ref, gr1_ref, x_ref,
                  wpa_ref, wpr_ref, wo_ref, g2_ref, h_ref, hn_ref):
    oa = oa_ref[...]
    orr = or_ref[...]
    h = x_ref[...]
    for n, (ga_ref, gr_ref) in enumerate(((ga0_ref, gr0_ref), (ga1_ref, gr1_ref))):
        cols = slice(n * REGION, (n + 1) * REGION)
        ya = jnp.dot(oa, wpa_ref[:, cols], preferred_element_type=F32)
        yr = jnp.dot(orr, wpr_ref[:, cols], preferred_element_type=F32)
        merged = ga_ref[...].astype(F32) * ya + gr_ref[...].astype(F32) * yr
        h = h + jnp.dot(merged.astype(BF16), wo_ref[cols, :], preferred_element_type=F32)
    h_ref[...] = h
    ms = jnp.mean(h * h, axis=-1, keepdims=True)
    hn_ref[...] = (h * lax.rsqrt(ms + NORM_EPS) * g2_ref[...]).astype(BF16)


def _merge(oa, orr, proj, x2d, wpa, wpr, wo, g2, tm):
    m = x2d.shape[0]
    row = lambda width: pl.BlockSpec((tm, width), lambda i: (i, 0))
    gate = lambda col: pl.BlockSpec((tm, REGION), lambda i: (i, col))
    whole = lambda shape: pl.BlockSpec(shape, lambda i: (0, 0), pipeline_mode=pl.Buffered(1))
    return pl.pallas_call(
        _merge_kernel,
        grid=(m // tm,),
        in_specs=[
            row(REGION), row(REGION),
            gate(COL_G_A), gate(COL_G_A + 1), gate(COL_G_R), gate(COL_G_R + 1),
            row(D_MODEL),
            whole((REGION, D_MODEL)), whole((REGION, D_MODEL)), whole((D_MODEL, D_MODEL)),
            whole((1, D_MODEL)),
        ],
        out_specs=[row(D_MODEL), row(D_MODEL)],
        out_shape=[jax.ShapeDtypeStruct((m, D_MODEL), F32),
                   jax.ShapeDtypeStruct((m, D_MODEL), BF16)],
        compiler_params=pltpu.CompilerParams(
            dimension_semantics=("arbitrary",),
            vmem_limit_bytes=VMEM_LIMIT),
        name="merge",
    )(oa, orr, proj, proj, proj, proj, x2d, wpa, wpr, wo, g2)


def _mlp_kernel(hn_ref, wup_ref, wdown_ref, h_ref, gf_ref, o_ref):
    f = pl.program_id(1)
    last = pl.num_programs(1) - 1

    def ffn_chunk():
        u = jnp.dot(hn_ref[...], wup_ref[...], preferred_element_type=F32)
        a = jnp.square(jnp.maximum(u, 0.0)).astype(BF16)
        return jnp.dot(a, wdown_ref[...], preferred_element_type=F32)

    @pl.when(f == 0)
    def _():
        o_ref[...] = h_ref[...] + ffn_chunk()

    @pl.when((f > 0) & (f < last))
    def _():
        o_ref[...] += ffn_chunk()

    @pl.when(f == last)
    def _():
        y = o_ref[...] + ffn_chunk()
        ms = jnp.mean(y * y, axis=-1, keepdims=True)
        o_ref[...] = y * lax.rsqrt(ms + NORM_EPS) * gf_ref[...]


def _mlp(hn, h1, wup, wdown, gf, tm, tf):
    m = hn.shape[0]
    return pl.pallas_call(
        _mlp_kernel,
        grid=(m // tm, D_FF // tf),
        in_specs=[
            pl.BlockSpec((tm, D_MODEL), lambda i, f: (i, 0)),
            pl.BlockSpec((D_MODEL, tf), lambda i, f: (0, f)),
            pl.BlockSpec((tf, D_MODEL), lambda i, f: (f, 0)),
            pl.BlockSpec((tm, D_MODEL), lambda i, f: (i, 0)),
            pl.BlockSpec((1, D_MODEL), lambda i, f: (0, 0)),
        ],
        out_specs=pl.BlockSpec((tm, D_MODEL), lambda i, f: (i, 0)),
        out_shape=jax.ShapeDtypeStruct((m, D_MODEL), F32),
        compiler_params=pltpu.CompilerParams(
            dimension_semantics=("arbitrary", "arbitrary"),
            vmem_limit_bytes=VMEM_LIMIT),
        name="mlp",
    )(hn, wup, wdown, h1, gf)


def _rope_table(pos):
    def inv_freq(half):
        return ROPE_THETA ** (-jnp.arange(half, dtype=F32) / half)

    inv_a, inv_r = inv_freq(DA_HEAD_DIM // 2), inv_freq(RET_QK_DIM // 2)
    ang = pos.astype(F32)[:, None] * jnp.concatenate([inv_a, inv_a, inv_r])[None, :]
    sign = jnp.concatenate([-jnp.ones_like(inv_a), jnp.ones_like(inv_a), jnp.ones_like(inv_r)])
    return jnp.concatenate([jnp.cos(ang), jnp.sin(ang) * sign[None, :]], axis=-1)


def kernel(x, meta_tokens, norm1_g, w_in, lam_q1, lam_k1, lam_q2, lam_k2, da_subln_g,
           w_pa, w_pr, w_o, norm2_g, w_up, w_down, normf_g):
    batch, seq, d = x.shape
    x2d = x.reshape(batch * seq, d)
    w_in_b = w_in[0].astype(BF16)
    g1 = norm1_g[0].reshape(1, d)

    tab_real = _rope_table(N_META + jnp.arange(seq))
    tab_meta = _rope_table(jnp.arange(N_META))

    main_tiles = tuple((r, r + 1 if r + 1 < N_REGIONS else None) for r in range(0, N_REGIONS, 2))
    proj, vt = _inproj(x2d, g1, w_in_b, tab_real, tm=1024, tiles=main_tiles, emit_vt=True)
    meta_tiles = ((COL_DA_K, COL_DA_V), (COL_R_K, COL_R_V))
    proj_meta, = _inproj(meta_tokens.astype(F32), g1, w_in_b, tab_meta, tm=N_META,
                         tiles=meta_tiles, emit_vt=False)
    meta_regions = tuple(r for tile in meta_tiles for r in tile)
    pad = ((0, META_PAD - N_META), (0, 0))

    def sl(col):
        at = meta_regions.index(col) * REGION
        return jnp.pad(proj_meta[:, at:at + REGION], pad)

    row = lambda a: a[0].reshape(1, -1)
    oa, w_pa_b, w_pr_b, w_o_b, w_up_b, w_down_b = _diff_attention(
        proj, vt, sl(COL_DA_K), sl(COL_DA_V).T, row(lam_q1), row(lam_k1), row(lam_q2),
        row(lam_k2), row(da_subln_g), batch, seq,
        cast=(w_pa[0], w_pr[0], w_o[0], w_up[0], w_down[0]))

    log_gamma = jnp.log(1.0 - 2.0 ** (-5.0 - jnp.arange(RET_HEADS, dtype=F32)))
    orr = _retention(log_gamma, proj, sl(COL_R_K), sl(COL_R_V), batch, seq, chunk=512)

    h1, hn = _merge(oa, orr, proj, x2d, w_pa_b, w_pr_b, w_o_b, row(norm2_g), tm=512)
    out = _mlp(hn, h1, w_up_b, w_down_b, normf_g.reshape(1, d), tm=512, tf=2048)
    return out.reshape(batch, seq, d)
```

```python
import functools
import math

import jax
import jax.numpy as jnp
from jax import lax
from jax.experimental import pallas as pl
from jax.experimental.pallas import tpu as pltpu

F32 = jnp.float32
BF16 = jnp.bfloat16

D_MODEL = 2048
N_META = 16
ROPE_THETA = 10000.0
NORM_EPS = 1e-6
DA_HEADS = 4
DA_HEAD_DIM = 128
DA_V_DIM = 2 * DA_HEAD_DIM
RET_HEADS = 4
RET_QK_DIM = 256
RET_V_DIM = 256
D_FF = 4 * D_MODEL
LAMBDA_INIT = 0.8 - 0.6 * math.exp(-0.3 * 0)

REGION = 1024
COL_DA_Q, COL_DA_K, COL_DA_V, COL_R_Q, COL_R_K, COL_R_V, COL_R_G, COL_G_A, COL_G_R = (
    0, 1, 2, 3, 4, 5, 6, 7, 9)
N_REGIONS = 11

LANES = 128
META_PAD = 128
ONES_ROWS = 16
NEG_BIG = -1e30
LOG2_E = math.log2(math.e)
VMEM_LIMIT = 58 * 1024 * 1024

NT_DIMS = (((1,), (1,)), ((), ()))
TN_DIMS = (((0,), (0,)), ((), ()))


def _sigmoid(x):
    return 0.5 + 0.5 * jnp.tanh(0.5 * x)


def _pick(t, values):
    r = values[0]
    for idx, v in enumerate(values[1:], 1):
        r = jnp.where(t >= idx, v, r)
    return r


def _inproj_kernel(x_hbm, g_ref, wa_ref, wb_ref, tab_ref, o_ref, *rest, tiles, emit_vt, tm):
    vt_ref = rest[0] if emit_vt else None
    xbuf, xn_ref, sem = rest[-3:]
    i = pl.program_id(0)
    t = pl.program_id(1)

    def x_copy(row_tile):
        rows = pl.ds(pl.multiple_of(row_tile * tm, tm), tm)
        return pltpu.make_async_copy(x_hbm.at[rows, :], xbuf, sem)

    @pl.when((t == 0) & (i == 0))
    def _():
        x_copy(0).start()

    @pl.when(t == 0)
    def _():
        x_copy(i).wait()
        x = xbuf[...]
        ms = jnp.mean(x * x, axis=-1, keepdims=True)
        xn_ref[...] = (x * lax.rsqrt(ms + NORM_EPS) * g_ref[...]).astype(BF16)

    @pl.when((t == 1) & (i + 1 < pl.num_programs(0)))
    def _():
        x_copy(i + 1).start()

    def scaled(tab, scale):
        return tab if scale == 1.0 else tab * scale

    def rope_da(acc, base, scale):
        cos = scaled(tab_ref[:, 0:LANES], scale)
        sin = scaled(tab_ref[:, 2 * LANES:3 * LANES], scale)
        for c in range(REGION // LANES):
            xc = acc[:, c * LANES:(c + 1) * LANES]
            r = xc * cos + pltpu.roll(xc, LANES // 2, 1) * sin
            o_ref[:, base + c * LANES:base + (c + 1) * LANES] = r.astype(BF16)

    def rope_ret(acc, base, scale):
        cos = scaled(tab_ref[:, LANES:2 * LANES], scale)
        sin = scaled(tab_ref[:, 3 * LANES:4 * LANES], scale)
        for h in range(RET_HEADS):
            lo = h * RET_QK_DIM
            x1 = acc[:, lo:lo + LANES]
            x2 = acc[:, lo + LANES:lo + 2 * LANES]
            o_ref[:, base + lo:base + lo + LANES] = (x1 * cos - x2 * sin).astype(BF16)
            o_ref[:, base + lo + LANES:base + lo + 2 * LANES] = (x2 * cos + x1 * sin).astype(BF16)

    def epilogue(region, acc, base):
        out = slice(base, base + REGION)
        if region == COL_DA_Q:
            rope_da(acc, base, DA_HEAD_DIM ** -0.5 * LOG2_E)
        elif region == COL_DA_K:
            rope_da(acc, base, 1.0)
        elif region == COL_R_Q:
            rope_ret(acc, base, 1.0)
        elif region == COL_R_K:
            rope_ret(acc, base, RET_QK_DIM ** -0.5)
        elif region == COL_DA_V:
            o_ref[:, out] = acc.astype(BF16)
            if emit_vt:
                vt_ref[0] = acc.T.astype(BF16)
        elif region == COL_R_V:
            o_ref[:, out] = acc.astype(BF16)
        elif region == COL_R_G:
            o_ref[:, out] = (acc * _sigmoid(acc)).astype(BF16)
        else:
            o_ref[:, out] = _sigmoid(acc).astype(BF16)

    for idx, tile in enumerate(tiles):
        @pl.when(t == idx)
        def _(tile=tile):
            for half, (region, w_ref) in enumerate(zip(tile, (wa_ref, wb_ref))):
                base = half * REGION
                if region is None:
                    o_ref[:, base:base + REGION] = jnp.zeros((tm, REGION), BF16)
                else:
                    acc = jnp.dot(xn_ref[...], w_ref[...], preferred_element_type=F32)
                    epilogue(region, acc, base)


def _inproj(x2d, g, w_bf16, tab, tm, tiles, emit_vt):
    m = x2d.shape[0]
    n_rows = m // tm
    n_tab = tab.shape[0] // tm
    assert len(tiles) >= 2
    first = tuple(a for a, _ in tiles)
    second = tuple(b if b is not None else tiles[idx - 1][1] for idx, (_, b) in enumerate(tiles))
    out_specs = [pl.BlockSpec((tm, 2 * REGION), lambda i, t: (i, t))]
    out_shape = [jax.ShapeDtypeStruct((m, len(tiles) * 2 * REGION), BF16)]
    if emit_vt:
        out_specs.append(pl.BlockSpec((1, REGION, tm), lambda i, t: (i, 0, 0)))
        out_shape.append(jax.ShapeDtypeStruct((n_rows, REGION, tm), BF16))
    return pl.pallas_call(
        functools.partial(_inproj_kernel, tiles=tiles, emit_vt=emit_vt, tm=tm),
        grid=(n_rows, len(tiles)),
        in_specs=[
            pl.BlockSpec(memory_space=pl.ANY),
            pl.BlockSpec((1, D_MODEL), lambda i, t: (0, 0)),
            pl.BlockSpec((D_MODEL, REGION), lambda i, t: (0, _pick(t, first))),
            pl.BlockSpec((D_MODEL, REGION), lambda i, t: (0, _pick(t, second))),
            pl.BlockSpec((tm, 4 * LANES), lambda i, t: (i % n_tab, 0)),
        ],
        out_specs=out_specs,
        out_shape=out_shape,
        scratch_shapes=[
            pltpu.VMEM((tm, D_MODEL), F32),
            pltpu.VMEM((tm, D_MODEL), BF16),
            pltpu.SemaphoreType.DMA(()),
        ],
        compiler_params=pltpu.CompilerParams(
            dimension_semantics=("arbitrary", "arbitrary"),
            vmem_limit_bytes=VMEM_LIMIT),
        name="inproj",
    )(x2d, g, w_bf16, w_bf16, tab)


def _attn_kernel(lq1_ref, lk1_ref, lq2_ref, lk2_ref, subg_ref, q_ref, k_ref, vt_ref,
                 km_ref, vmt_ref, *rest, tq, n_q_tiles, n_cast, running_max):
    cast_in = rest[:n_cast]
    o_ref, finite_ref = rest[n_cast:n_cast + 2]
    cast_out = rest[n_cast + 2:2 * n_cast + 2]
    m_scr, l_scr, acc_scr = rest[2 * n_cast + 2:]
    qi = pl.program_id(2)
    half = tq // 2
    every = slice(None)

    for src, dst in zip(cast_in, cast_out):
        dst[...] = src[...].astype(BF16)

    def scores(k_blk, queries):
        out = []
        for c in range(2):
            lanes = slice(c * DA_HEAD_DIM, (c + 1) * DA_HEAD_DIM)
            out.append(lax.dot_general(k_blk[:, lanes], q_ref[queries, lanes], NT_DIMS,
                                       preferred_element_type=F32))
        return out

    def absorb(block_scores, vt_blk, queries, mask, first):
        vt = vt_blk[...]
        vt = jnp.concatenate([vt, jnp.ones((ONES_ROWS, vt.shape[1]), BF16)], axis=0)
        for c, s in enumerate(block_scores):
            if mask is not None:
                s = jnp.where(mask, s, NEG_BIG)
            alpha = None
            if first:
                m_ref = jnp.max(s, axis=0, keepdims=True)
                m_scr[c, :, queries] = m_ref
            elif running_max:
                m_old = m_scr[c, :, queries]
                m_ref = jnp.maximum(m_old, jnp.max(s, axis=0, keepdims=True))
                alpha = jnp.exp2(m_old - m_ref)
                m_scr[c, :, queries] = m_ref
            else:
                m_ref = m_scr[c, :, queries]
            p = jnp.exp2(s - m_ref).astype(BF16)
            pv_ext = jnp.dot(vt, p, preferred_element_type=F32)
            pv = pv_ext[0:DA_V_DIM]
            psum = pv_ext[DA_V_DIM:DA_V_DIM + 1]
            if first:
                l_scr[c, :, queries] = psum
                acc_scr[c, :, queries] = pv
            elif alpha is None:
                l_scr[c, :, queries] += psum
                acc_scr[c, :, queries] += pv
            else:
                l_scr[c, :, queries] = alpha * l_scr[c, :, queries] + psum
                acc_scr[c, :, queries] = alpha * acc_scr[c, :, queries] + pv

    def causal(n_q):
        key = lax.broadcasted_iota(jnp.int32, (half, n_q), 0)
        query = lax.broadcasted_iota(jnp.int32, (half, n_q), 1)
        return key <= query

    def key_blocks(n_full):
        meta_mask = lax.broadcasted_iota(jnp.int32, (META_PAD, tq), 0) < N_META
        blocks = [(km_ref, vmt_ref, every, meta_mask, True)]
        for j in range(n_full):
            blocks.append((k_ref.at[j * tq:(j + 1) * tq, :], vt_ref.at[j], every, None, False))
        diag = n_full * tq
        blocks.append((k_ref.at[diag:diag + half, :], vt_ref.at[n_full, :, 0:half], every,
                       causal(tq), False))
        blocks.append((k_ref.at[diag + half:diag + tq, :], vt_ref.at[n_full, :, half:tq],
                       slice(half, tq), causal(half), False))
        return blocks

    for n_full in range(n_q_tiles):
        @pl.when(qi == n_full)
        def _(n_full=n_full):
            blocks = key_blocks(n_full)
            ahead = scores(blocks[0][0], blocks[0][2])
            for t, (_, vt_blk, queries, mask, first) in enumerate(blocks):
                current = ahead
                if t + 1 < len(blocks):
                    ahead = scores(blocks[t + 1][0], blocks[t + 1][2])
                absorb(current, vt_blk, queries, mask, first)

    lam = (jnp.exp(jnp.sum(lq1_ref[...] * lk1_ref[...], axis=-1, keepdims=True))
           - jnp.exp(jnp.sum(lq2_ref[...] * lk2_ref[...], axis=-1, keepdims=True))
           + LAMBDA_INIT)
    l1, l2 = l_scr[0], l_scr[1]
    o_t = acc_scr[0] * (1.0 / l1) - acc_scr[1] * (lam / l2)
    o = o_t.T
    ms = jnp.mean(o * o, axis=-1, keepdims=True)
    o = o * lax.rsqrt(ms + NORM_EPS) * subg_ref[...] * (1.0 - LAMBDA_INIT)
    o_ref[...] = o.astype(BF16)
    big = jnp.finfo(F32).max
    finite_l = jnp.where((l1 <= big) & (l2 <= big), 1.0, 0.0)
    finite_o = jnp.min(jnp.where(jnp.abs(o_t) <= big, 1.0, 0.0), axis=0, keepdims=True)
    finite = jnp.min(jnp.minimum(finite_l, finite_o), axis=1, keepdims=True)
    finite_ref[...] = jnp.broadcast_to(finite, finite_ref.shape)


def _diff_attention(proj, vt, meta_k, meta_vt, lq1, lk1, lq2, lk2, subg, batch, seq, cast,
                    running_max):
    tq = vt.shape[2]
    nq = seq // tq
    n_steps = batch * DA_HEADS * nq
    step = lambda b, h, i: (b * DA_HEADS + h) * nq + i
    vec = pl.BlockSpec((1, DA_HEAD_DIM), lambda b, h, i: (0, 0))
    per_head = REGION // DA_V_DIM
    cast_specs = [pl.BlockSpec((a.shape[0] // n_steps, a.shape[1]),
                               lambda b, h, i: (step(b, h, i), 0)) for a in cast]
    flag_tile = (8, LANES)
    return pl.pallas_call(
        functools.partial(_attn_kernel, tq=tq, n_q_tiles=nq, n_cast=len(cast),
                          running_max=running_max),
        grid=(batch, DA_HEADS, nq),
        in_specs=[
            vec, vec, vec, vec,
            pl.BlockSpec((1, DA_V_DIM), lambda b, h, i: (0, 0)),
            pl.BlockSpec((tq, DA_V_DIM), lambda b, h, i: (b * nq + i, COL_DA_Q * per_head + h)),
            pl.BlockSpec((seq, DA_V_DIM), lambda b, h, i: (b, COL_DA_K * per_head + h)),
            pl.BlockSpec((nq, DA_V_DIM, tq), lambda b, h, i: (b, h, 0)),
            pl.BlockSpec((META_PAD, DA_V_DIM), lambda b, h, i: (0, h)),
            pl.BlockSpec((DA_V_DIM, META_PAD), lambda b, h, i: (h, 0)),
        ] + cast_specs,
        out_specs=[pl.BlockSpec((tq, DA_V_DIM), lambda b, h, i: (b * nq + i, h)),
                   pl.BlockSpec((1,) + flag_tile, lambda b, h, i: (step(b, h, i), 0, 0))]
        + cast_specs,
        out_shape=[jax.ShapeDtypeStruct((batch * seq, DA_HEADS * DA_V_DIM), BF16),
                   jax.ShapeDtypeStruct((n_steps,) + flag_tile, F32)]
        + [jax.ShapeDtypeStruct(a.shape, BF16) for a in cast],
        scratch_shapes=[
            pltpu.VMEM((2, 1, tq), F32),
            pltpu.VMEM((2, 1, tq), F32),
            pltpu.VMEM((2, DA_V_DIM, tq), F32),
        ],
        compiler_params=pltpu.CompilerParams(
            dimension_semantics=("arbitrary", "arbitrary", "arbitrary"),
            vmem_limit_bytes=VMEM_LIMIT),
        name="diff_attention",
    )(lq1, lk1, lq2, lk2, subg, proj, proj, vt, meta_k, meta_vt, *cast)


def _ret_kernel(lg_ref, q_ref, k_ref, v_ref, g_ref, km_ref, vm_ref, o_ref,
                state_scr, decay_scr, xi_scr, zeta_scr, *, chunk):
    b = pl.program_id(0)
    c = pl.program_id(1)
    heads = [(h, lg_ref[h], slice(h * RET_V_DIM, (h + 1) * RET_V_DIM))
             for h in range(RET_HEADS)]

    @pl.when((b == 0) & (c == 0))
    def _():
        row = lax.broadcasted_iota(jnp.int32, (chunk, chunk), 0)
        col = lax.broadcasted_iota(jnp.int32, (chunk, chunk), 1)
        rel = (row - col).astype(F32)
        idx = lax.broadcasted_iota(jnp.int32, (chunk, RET_V_DIM), 0).astype(F32)
        for h, lg, _ in heads:
            decay_scr[h] = jnp.where(rel >= 0, jnp.exp(lg * jnp.maximum(rel, 0.0)), 0.0)
            xi_scr[h] = jnp.exp(lg * (idx + 1.0))
            zeta_scr[h] = jnp.exp(lg * (chunk - 1.0 - idx))

    @pl.when(c == 0)
    def _():
        midx = lax.broadcasted_iota(jnp.int32, (META_PAD, RET_V_DIM), 0).astype(F32)
        for h, lg, cols in heads:
            mz = jnp.exp(lg * (N_META - 1.0 - midx))
            vz = (vm_ref[:, cols].astype(F32) * mz).astype(BF16)
            state_scr[h] = lax.dot_general(km_ref[:, cols], vz, TN_DIMS,
                                           preferred_element_type=F32)

    for h, lg, cols in heads:
        q = q_ref[:, cols]
        k = k_ref[:, cols]
        v = v_ref[:, cols]
        s = lax.dot_general(q, k, NT_DIMS, preferred_element_type=F32) * decay_scr[h]
        inner = jnp.dot(s.astype(BF16), v, preferred_element_type=F32)
        state = state_scr[h]
        cross = jnp.dot(q, state.astype(BF16), preferred_element_type=F32) * xi_scr[h]
        o = inner + cross
        vz = (v.astype(F32) * zeta_scr[h]).astype(BF16)
        chunk_decay = jnp.exp(jnp.full((1, 1), lg * chunk, F32))
        state_scr[h] = chunk_decay * state + lax.dot_general(
            k, vz, TN_DIMS, preferred_element_type=F32)
        ms = jnp.mean(o * o, axis=-1, keepdims=True)
        o_ref[:, cols] = (o * lax.rsqrt(ms + NORM_EPS)
                          * g_ref[:, cols].astype(F32)).astype(BF16)


def _retention(log_gamma, proj, meta_k, meta_v, batch, seq, chunk):
    nc = seq // chunk
    tok = lambda col: pl.BlockSpec((chunk, REGION), lambda b, c, lg: (b * nc + c, col))
    meta = pl.BlockSpec((META_PAD, REGION), lambda b, c, lg: (0, 0))
    grid_spec = pltpu.PrefetchScalarGridSpec(
        num_scalar_prefetch=1,
        grid=(batch, nc),
        in_specs=[tok(COL_R_Q), tok(COL_R_K), tok(COL_R_V), tok(COL_R_G), meta, meta],
        out_specs=pl.BlockSpec((chunk, REGION), lambda b, c, lg: (b * nc + c, 0)),
        scratch_shapes=[
            pltpu.VMEM((RET_HEADS, RET_QK_DIM, RET_V_DIM), F32),
            pltpu.VMEM((RET_HEADS, chunk, chunk), F32),
            pltpu.VMEM((RET_HEADS, chunk, RET_V_DIM), F32),
            pltpu.VMEM((RET_HEADS, chunk, RET_V_DIM), F32),
        ],
    )
    return pl.pallas_call(
        functools.partial(_ret_kernel, chunk=chunk),
        grid_spec=grid_spec,
        out_shape=jax.ShapeDtypeStruct((batch * seq, RET_HEADS * RET_V_DIM), BF16),
        compiler_params=pltpu.CompilerParams(
            dimension_semantics=("arbitrary", "arbitrary"),
            vmem_limit_bytes=VMEM_LIMIT),
        name="retention",
    )(log_gamma, proj, proj, proj, proj, meta_k, meta_v)


def _merge_kernel(oa_ref, or_ref, ga0_ref, ga1_ref, gr0_ref, gr1_ref, x_ref,
                  wpa_ref, wpr_ref, wo_ref, g2_ref, h_ref, hn_ref):
    oa = oa_ref[...]
    orr = or_ref[...]
    h = x_ref[...]
    for n, (ga_ref, gr_ref) in enumerate(((ga0_ref, gr0_ref), (ga1_ref, gr1_ref))):
        cols = slice(n * REGION, (n + 1) * REGION)
        ya = jnp.dot(oa, wpa_ref[:, cols], preferred_element_type=F32)
        yr = jnp.dot(orr, wpr_ref[:, cols], preferred_element_type=F32)
        merged = ga_ref[...].astype(F32) * ya + gr_ref[...].astype(F32) * yr
        h = h + jnp.dot(merged.astype(BF16), wo_ref[cols, :], preferred_element_type=F32)
    h_ref[...] = h
    ms = jnp.mean(h * h, axis=-1, keepdims=True)
    hn_ref[...] = (h * lax.rsqrt(ms + NORM_EPS) * g2_ref[...]).astype(BF16)


def _merge(oa, orr, proj, x2d, wpa, wpr, wo, g2, tm):
    m = x2d.shape[0]
    row = lambda width: pl.BlockSpec((tm, width), lambda i: (i, 0))
    gate = lambda col: pl.BlockSpec((tm, REGION), lambda i: (i, col))
    whole = lambda shape: pl.BlockSpec(shape, lambda i: (0, 0), pipeline_mode=pl.Buffered(1))
    return pl.pallas_call(
        _merge_kernel,
        grid=(m // tm,),
        in_specs=[
            row(REGION), row(REGION),
            gate(COL_G_A), gate(COL_G_A + 1), gate(COL_G_R), gate(COL_G_R + 1),
            row(D_MODEL),
            whole((REGION, D_MODEL)), whole((REGION, D_MODEL)), whole((D_MODEL, D_MODEL)),
            whole((1, D_MODEL)),
        ],
        out_specs=[row(D_MODEL), row(D_MODEL)],
        out_shape=[jax.ShapeDtypeStruct((m, D_MODEL), F32),
                   jax.ShapeDtypeStruct((m, D_MODEL), BF16)],
        compiler_params=pltpu.CompilerParams(
            dimension_semantics=("arbitrary",),
            vmem_limit_bytes=VMEM_LIMIT),
        name="merge",
    )(oa, orr, proj, proj, proj, proj, x2d, wpa, wpr, wo, g2)


def _mlp_kernel(hn_ref, wup_ref, wdown_ref, h_ref, gf_ref, o_ref):
    f = pl.program_id(1)
    last = pl.num_programs(1) - 1

    def ffn_chunk():
        u = jnp.dot(hn_ref[...], wup_ref[...], preferred_element_type=F32)
        a = jnp.square(jnp.maximum(u, 0.0)).astype(BF16)
        return jnp.dot(a, wdown_ref[...], preferred_element_type=F32)

    @pl.when(f == 0)
    def _():
        o_ref[...] = h_ref[...] + ffn_chunk()

    @pl.when((f > 0) & (f < last))
    def _():
        o_ref[...] += ffn_chunk()

    @pl.when(f == last)
    def _():
        y = o_ref[...] + ffn_chunk()
        ms = jnp.mean(y * y, axis=-1, keepdims=True)
        o_ref[...] = y * lax.rsqrt(ms + NORM_EPS) * gf_ref[...]


def _mlp(hn, h1, wup, wdown, gf, tm, tf):
    m = hn.shape[0]
    return pl.pallas_call(
        _mlp_kernel,
        grid=(m // tm, D_FF // tf),
        in_specs=[
            pl.BlockSpec((tm, D_MODEL), lambda i, f: (i, 0)),
            pl.BlockSpec((D_MODEL, tf), lambda i, f: (0, f)),
            pl.BlockSpec((tf, D_MODEL), lambda i, f: (f, 0)),
            pl.BlockSpec((tm, D_MODEL), lambda i, f: (i, 0)),
            pl.BlockSpec((1, D_MODEL), lambda i, f: (0, 0)),
        ],
        out_specs=pl.BlockSpec((tm, D_MODEL), lambda i, f: (i, 0)),
        out_shape=jax.ShapeDtypeStruct((m, D_MODEL), F32),
        compiler_params=pltpu.CompilerParams(
            dimension_semantics=("arbitrary", "arbitrary"),
            vmem_limit_bytes=VMEM_LIMIT),
        name="mlp",
    )(hn, wup, wdown, h1, gf)


def _rope_table(pos):
    def inv_freq(half):
        return ROPE_THETA ** (-jnp.arange(half, dtype=F32) / half)

    inv_a, inv_r = inv_freq(DA_HEAD_DIM // 2), inv_freq(RET_QK_DIM // 2)
    ang = pos.astype(F32)[:, None] * jnp.concatenate([inv_a, inv_a, inv_r])[None, :]
    sign = jnp.concatenate([-jnp.ones_like(inv_a), jnp.ones_like(inv_a), jnp.ones_like(inv_r)])
    return jnp.concatenate([jnp.cos(ang), jnp.sin(ang) * sign[None, :]], axis=-1)


def kernel(x, meta_tokens, norm1_g, w_in, lam_q1, lam_k1, lam_q2, lam_k2, da_subln_g,
           w_pa, w_pr, w_o, norm2_g, w_up, w_down, normf_g):
    batch, seq, d = x.shape
    x2d = x.reshape(batch * seq, d)
    w_in_b = w_in[0].astype(BF16)
    g1 = norm1_g[0].reshape(1, d)

    tab_real = _rope_table(N_META + jnp.arange(seq))
    tab_meta = _rope_table(jnp.arange(N_META))

    main_tiles = tuple((r, r + 1 if r + 1 < N_REGIONS else None) for r in range(0, N_REGIONS, 2))
    proj, vt = _inproj(x2d, g1, w_in_b, tab_real, tm=1024, tiles=main_tiles, emit_vt=True)
    meta_tiles = ((COL_DA_K, COL_DA_V), (COL_R_K, COL_R_V))
    proj_meta, = _inproj(meta_tokens.astype(F32), g1, w_in_b, tab_meta, tm=N_META,
                         tiles=meta_tiles, emit_vt=False)
    meta_regions = tuple(r for tile in meta_tiles for r in tile)
    pad = ((0, META_PAD - N_META), (0, 0))

    def sl(col):
        at = meta_regions.index(col) * REGION
        return jnp.pad(proj_meta[:, at:at + REGION], pad)

    row = lambda a: a[0].reshape(1, -1)
    attention = functools.partial(
        _diff_attention, proj, vt, sl(COL_DA_K), sl(COL_DA_V).T, row(lam_q1), row(lam_k1),
        row(lam_q2), row(lam_k2), row(da_subln_g), batch, seq)
    oa, finite, w_pa_b, w_pr_b, w_o_b, w_up_b, w_down_b = attention(
        cast=(w_pa[0], w_pr[0], w_o[0], w_up[0], w_down[0]), running_max=False)
    oa = lax.cond(jnp.min(finite) > 0.5, lambda: oa,
                  lambda: attention(cast=(), running_max=True)[0])

    log_gamma = jnp.log(1.0 - 2.0 ** (-5.0 - jnp.arange(RET_HEADS, dtype=F32)))
    orr = _retention(log_gamma, proj, sl(COL_R_K), sl(COL_R_V), batch, seq, chunk=512)

    h1, hn = _merge(oa, orr, proj, x2d, w_pa_b, w_pr_b, w_o_b, row(norm2_g), tm=512)
    out = _mlp(hn, h1, w_up_b, w_down_b, normf_g.reshape(1, d), tm=512, tf=2048)
    return out.reshape(batch, seq, d)
```

```python
import functools
import math

import jax
import jax.numpy as jnp
from jax import lax
from jax.experimental import pallas as pl
from jax.experimental.pallas import tpu as pltpu

F32 = jnp.float32
BF16 = jnp.bfloat16

D_MODEL = 2048
N_META = 16
ROPE_THETA = 10000.0
NORM_EPS = 1e-6
DA_HEADS = 4
DA_HEAD_DIM = 128
DA_V_DIM = 2 * DA_HEAD_DIM
RET_HEADS = 4
RET_QK_DIM = 256
RET_V_DIM = 256
D_FF = 4 * D_MODEL
LAMBDA_INIT = 0.8 - 0.6 * math.exp(-0.3 * 0)

REGION = 1024
COL_DA_Q, COL_DA_K, COL_DA_V, COL_R_Q, COL_R_K, COL_R_V, COL_R_G, COL_G_A, COL_G_R = (
    0, 1, 2, 3, 4, 5, 6, 7, 9)
N_REGIONS = 11

LANES = 128
META_PAD = 128
ONES_ROWS = 16
NEG_BIG = -1e30
LOG2_E = math.log2(math.e)
VMEM_LIMIT = 58 * 1024 * 1024

INPROJ_ROWS = 1024
RET_CHUNK = 512
MERGE_ROWS = 512
MLP_ROWS = 512
MLP_FF_TILE = 2048

NT_DIMS = (((1,), (1,)), ((), ()))
TN_DIMS = (((0,), (0,)), ((), ()))


def _sigmoid(x):
    return 0.5 + 0.5 * jnp.tanh(0.5 * x)


def _pick(t, values):
    r = values[0]
    for idx, v in enumerate(values[1:], 1):
        r = jnp.where(t >= idx, v, r)
    return r


def _inproj_kernel(x_hbm, g_ref, wa_ref, wb_ref, tab_ref, o_ref, *rest, tiles, emit_vt, tm):
    vt_ref = rest[0] if emit_vt else None
    xbuf, xn_ref, sem = rest[-3:]
    i = pl.program_id(0)
    t = pl.program_id(1)

    def x_copy(row_tile):
        rows = pl.ds(pl.multiple_of(row_tile * tm, tm), tm)
        return pltpu.make_async_copy(x_hbm.at[rows, :], xbuf, sem)

    @pl.when((t == 0) & (i == 0))
    def _():
        x_copy(0).start()

    @pl.when(t == 0)
    def _():
        x_copy(i).wait()
        x = xbuf[...]
        ms = jnp.mean(x * x, axis=-1, keepdims=True)
        xn_ref[...] = (x * lax.rsqrt(ms + NORM_EPS) * g_ref[...]).astype(BF16)

    @pl.when((t == 1) & (i + 1 < pl.num_programs(0)))
    def _():
        x_copy(i + 1).start()

    def scaled(tab, scale):
        return tab if scale == 1.0 else tab * scale

    def rope_da(acc, base, scale):
        cos = scaled(tab_ref[:, 0:LANES], scale)
        sin = scaled(tab_ref[:, 2 * LANES:3 * LANES], scale)
        for c in range(REGION // LANES):
            xc = acc[:, c * LANES:(c + 1) * LANES]
            r = xc * cos + pltpu.roll(xc, LANES // 2, 1) * sin
            o_ref[:, base + c * LANES:base + (c + 1) * LANES] = r.astype(BF16)

    def rope_ret(acc, base, scale):
        cos = scaled(tab_ref[:, LANES:2 * LANES], scale)
        sin = scaled(tab_ref[:, 3 * LANES:4 * LANES], scale)
        for h in range(RET_HEADS):
            lo = h * RET_QK_DIM
            x1 = acc[:, lo:lo + LANES]
            x2 = acc[:, lo + LANES:lo + 2 * LANES]
            o_ref[:, base + lo:base + lo + LANES] = (x1 * cos - x2 * sin).astype(BF16)
            o_ref[:, base + lo + LANES:base + lo + 2 * LANES] = (x2 * cos + x1 * sin).astype(BF16)

    def epilogue(region, acc, base):
        out = slice(base, base + REGION)
        if region == COL_DA_Q:
            rope_da(acc, base, DA_HEAD_DIM ** -0.5 * LOG2_E)
        elif region == COL_DA_K:
            rope_da(acc, base, 1.0)
        elif region == COL_R_Q:
            rope_ret(acc, base, 1.0)
        elif region == COL_R_K:
            rope_ret(acc, base, RET_QK_DIM ** -0.5)
        elif region == COL_DA_V:
            o_ref[:, out] = acc.astype(BF16)
            if emit_vt:
                vt_ref[0] = acc.T.astype(BF16)
        elif region == COL_R_V:
            o_ref[:, out] = acc.astype(BF16)
        elif region == COL_R_G:
            o_ref[:, out] = (acc * _sigmoid(acc)).astype(BF16)
        else:
            o_ref[:, out] = _sigmoid(acc).astype(BF16)

    for idx, tile in enumerate(tiles):
        @pl.when(t == idx)
        def _(tile=tile):
            for half, (region, w_ref) in enumerate(zip(tile, (wa_ref, wb_ref))):
                base = half * REGION
                if region is None:
                    o_ref[:, base:base + REGION] = jnp.zeros((tm, REGION), BF16)
                else:
                    acc = jnp.dot(xn_ref[...], w_ref[...], preferred_element_type=F32)
                    epilogue(region, acc, base)


def _inproj(x2d, g, w_bf16, tab, tm, tiles, emit_vt):
    m = x2d.shape[0]
    n_rows = m // tm
    n_tab = tab.shape[0] // tm
    assert len(tiles) >= 2
    first = tuple(a for a, _ in tiles)
    second = tuple(b if b is not None else tiles[idx - 1][1] for idx, (_, b) in enumerate(tiles))
    out_specs = [pl.BlockSpec((tm, 2 * REGION), lambda i, t: (i, t))]
    out_shape = [jax.ShapeDtypeStruct((m, len(tiles) * 2 * REGION), BF16)]
    if emit_vt:
        out_specs.append(pl.BlockSpec((1, REGION, tm), lambda i, t: (i, 0, 0)))
        out_shape.append(jax.ShapeDtypeStruct((n_rows, REGION, tm), BF16))
    return pl.pallas_call(
        functools.partial(_inproj_kernel, tiles=tiles, emit_vt=emit_vt, tm=tm),
        grid=(n_rows, len(tiles)),
        in_specs=[
            pl.BlockSpec(memory_space=pl.ANY),
            pl.BlockSpec((1, D_MODEL), lambda i, t: (0, 0)),
            pl.BlockSpec((D_MODEL, REGION), lambda i, t: (0, _pick(t, first))),
            pl.BlockSpec((D_MODEL, REGION), lambda i, t: (0, _pick(t, second))),
            pl.BlockSpec((tm, 4 * LANES), lambda i, t: (i % n_tab, 0)),
        ],
        out_specs=out_specs,
        out_shape=out_shape,
        scratch_shapes=[
            pltpu.VMEM((tm, D_MODEL), F32),
            pltpu.VMEM((tm, D_MODEL), BF16),
            pltpu.SemaphoreType.DMA(()),
        ],
        compiler_params=pltpu.CompilerParams(
            dimension_semantics=("arbitrary", "arbitrary"),
            vmem_limit_bytes=VMEM_LIMIT),
        name="inproj",
    )(x2d, g, w_bf16, w_bf16, tab)


def _attn_kernel(lq1_ref, lk1_ref, lq2_ref, lk2_ref, subg_ref, q_ref, k_ref, vt_ref,
                 km_ref, vmt_ref, *rest, tq, n_q_tiles, n_cast, running_max):
    cast_in = rest[:n_cast]
    o_ref, finite_ref = rest[n_cast:n_cast + 2]
    cast_out = rest[n_cast + 2:2 * n_cast + 2]
    m_scr, l_scr, acc_scr = rest[2 * n_cast + 2:]
    qi = pl.program_id(2)
    half = tq // 2
    every = slice(None)

    for src, dst in zip(cast_in, cast_out):
        dst[...] = src[...].astype(BF16)

    def scores(k_blk, queries):
        out = []
        for c in range(2):
            lanes = slice(c * DA_HEAD_DIM, (c + 1) * DA_HEAD_DIM)
            out.append(lax.dot_general(k_blk[:, lanes], q_ref[queries, lanes], NT_DIMS,
                                       preferred_element_type=F32))
        return out

    def absorb(block_scores, vt_blk, queries, mask, first):
        vt = vt_blk[...]
        vt = jnp.concatenate([vt, jnp.ones((ONES_ROWS, vt.shape[1]), BF16)], axis=0)
        for c, s in enumerate(block_scores):
            if mask is not None:
                s = jnp.where(mask, s, NEG_BIG)
            alpha = None
            if first:
                m_ref = jnp.max(s, axis=0, keepdims=True)
                m_scr[c, :, queries] = m_ref
            elif running_max:
                m_old = m_scr[c, :, queries]
                m_ref = jnp.maximum(m_old, jnp.max(s, axis=0, keepdims=True))
                alpha = jnp.exp2(m_old - m_ref)
                m_scr[c, :, queries] = m_ref
            else:
                m_ref = m_scr[c, :, queries]
            p = jnp.exp2(s - m_ref).astype(BF16)
            pv_ext = jnp.dot(vt, p, preferred_element_type=F32)
            pv = pv_ext[0:DA_V_DIM]
            psum = pv_ext[DA_V_DIM:DA_V_DIM + 1]
            if first:
                l_scr[c, :, queries] = psum
                acc_scr[c, :, queries] = pv
            elif alpha is None:
                l_scr[c, :, queries] += psum
                acc_scr[c, :, queries] += pv
            else:
                l_scr[c, :, queries] = alpha * l_scr[c, :, queries] + psum
                acc_scr[c, :, queries] = alpha * acc_scr[c, :, queries] + pv

    def causal(n_q):
        key = lax.broadcasted_iota(jnp.int32, (half, n_q), 0)
        query = lax.broadcasted_iota(jnp.int32, (half, n_q), 1)
        return key <= query

    def key_blocks(n_full):
        meta_mask = lax.broadcasted_iota(jnp.int32, (META_PAD, tq), 0) < N_META
        blocks = [(km_ref, vmt_ref, every, meta_mask, True)]
        for j in range(n_full):
            blocks.append((k_ref.at[j * tq:(j + 1) * tq, :], vt_ref.at[j], every, None, False))
        diag = n_full * tq
        blocks.append((k_ref.at[diag:diag + half, :], vt_ref.at[n_full, :, 0:half], every,
                       causal(tq), False))
        blocks.append((k_ref.at[diag + half:diag + tq, :], vt_ref.at[n_full, :, half:tq],
                       slice(half, tq), causal(half), False))
        return blocks

    for n_full in range(n_q_tiles):
        @pl.when(qi == n_full)
        def _(n_full=n_full):
            blocks = key_blocks(n_full)
            ahead = scores(blocks[0][0], blocks[0][2])
            for t, (_, vt_blk, queries, mask, first) in enumerate(blocks):
                current = ahead
                if t + 1 < len(blocks):
                    ahead = scores(blocks[t + 1][0], blocks[t + 1][2])
                absorb(current, vt_blk, queries, mask, first)

    lam = (jnp.exp(jnp.sum(lq1_ref[...] * lk1_ref[...], axis=-1, keepdims=True))
           - jnp.exp(jnp.sum(lq2_ref[...] * lk2_ref[...], axis=-1, keepdims=True))
           + LAMBDA_INIT)
    l1, l2 = l_scr[0], l_scr[1]
    o_t = acc_scr[0] * (1.0 / l1) - acc_scr[1] * (lam / l2)
    o = o_t.T
    ms = jnp.mean(o * o, axis=-1, keepdims=True)
    o = o * lax.rsqrt(ms + NORM_EPS) * subg_ref[...] * (1.0 - LAMBDA_INIT)
    o_ref[...] = o.astype(BF16)
    big = jnp.finfo(F32).max
    finite_l = jnp.where((l1 <= big) & (l2 <= big), 1.0, 0.0)
    finite_o = jnp.min(jnp.where(jnp.abs(o_t) <= big, 1.0, 0.0), axis=0, keepdims=True)
    finite = jnp.min(jnp.minimum(finite_l, finite_o), axis=1, keepdims=True)
    finite_ref[...] = jnp.broadcast_to(finite, finite_ref.shape)


def _diff_attention(proj, vt, meta_k, meta_vt, lq1, lk1, lq2, lk2, subg, batch, seq, cast,
                    running_max):
    tq = vt.shape[2]
    nq = seq // tq
    n_steps = batch * DA_HEADS * nq
    step = lambda b, h, i: (b * DA_HEADS + h) * nq + i
    vec = pl.BlockSpec((1, DA_HEAD_DIM), lambda b, h, i: (0, 0))
    per_head = REGION // DA_V_DIM
    cast_specs = [pl.BlockSpec((a.shape[0] // n_steps, a.shape[1]),
                               lambda b, h, i: (step(b, h, i), 0)) for a in cast]
    flag_tile = (8, LANES)
    return pl.pallas_call(
        functools.partial(_attn_kernel, tq=tq, n_q_tiles=nq, n_cast=len(cast),
                          running_max=running_max),
        grid=(batch, DA_HEADS, nq),
        in_specs=[
            vec, vec, vec, vec,
            pl.BlockSpec((1, DA_V_DIM), lambda b, h, i: (0, 0)),
            pl.BlockSpec((tq, DA_V_DIM), lambda b, h, i: (b * nq + i, COL_DA_Q * per_head + h)),
            pl.BlockSpec((seq, DA_V_DIM), lambda b, h, i: (b, COL_DA_K * per_head + h)),
            pl.BlockSpec((nq, DA_V_DIM, tq), lambda b, h, i: (b, h, 0)),
            pl.BlockSpec((META_PAD, DA_V_DIM), lambda b, h, i: (0, h)),
            pl.BlockSpec((DA_V_DIM, META_PAD), lambda b, h, i: (h, 0)),
        ] + cast_specs,
        out_specs=[pl.BlockSpec((tq, DA_V_DIM), lambda b, h, i: (b * nq + i, h)),
                   pl.BlockSpec((1,) + flag_tile, lambda b, h, i: (step(b, h, i), 0, 0))]
        + cast_specs,
        out_shape=[jax.ShapeDtypeStruct((batch * seq, DA_HEADS * DA_V_DIM), BF16),
                   jax.ShapeDtypeStruct((n_steps,) + flag_tile, F32)]
        + [jax.ShapeDtypeStruct(a.shape, BF16) for a in cast],
        scratch_shapes=[
            pltpu.VMEM((2, 1, tq), F32),
            pltpu.VMEM((2, 1, tq), F32),
            pltpu.VMEM((2, DA_V_DIM, tq), F32),
        ],
        compiler_params=pltpu.CompilerParams(
            dimension_semantics=("arbitrary", "arbitrary", "arbitrary"),
            vmem_limit_bytes=VMEM_LIMIT),
        name="diff_attention",
    )(lq1, lk1, lq2, lk2, subg, proj, proj, vt, meta_k, meta_vt, *cast)


def _ret_kernel(lg_ref, q_ref, k_ref, v_ref, g_ref, km_ref, vm_ref, o_ref,
                state_scr, decay_scr, xi_scr, zeta_scr, *, chunk):
    b = pl.program_id(0)
    c = pl.program_id(1)
    heads = [(h, lg_ref[h], slice(h * RET_V_DIM, (h + 1) * RET_V_DIM))
             for h in range(RET_HEADS)]

    @pl.when((b == 0) & (c == 0))
    def _():
        row = lax.broadcasted_iota(jnp.int32, (chunk, chunk), 0)
        col = lax.broadcasted_iota(jnp.int32, (chunk, chunk), 1)
        rel = (row - col).astype(F32)
        idx = lax.broadcasted_iota(jnp.int32, (chunk, RET_V_DIM), 0).astype(F32)
        for h, lg, _ in heads:
            decay_scr[h] = jnp.where(rel >= 0, jnp.exp(lg * jnp.maximum(rel, 0.0)), 0.0)
            xi_scr[h] = jnp.exp(lg * (idx + 1.0))
            zeta_scr[h] = jnp.exp(lg * (chunk - 1.0 - idx))

    @pl.when(c == 0)
    def _():
        midx = lax.broadcasted_iota(jnp.int32, (META_PAD, RET_V_DIM), 0).astype(F32)
        for h, lg, cols in heads:
            mz = jnp.exp(lg * (N_META - 1.0 - midx))
            vz = (vm_ref[:, cols].astype(F32) * mz).astype(BF16)
            state_scr[h] = lax.dot_general(km_ref[:, cols], vz, TN_DIMS,
                                           preferred_element_type=F32)

    for h, lg, cols in heads:
        q = q_ref[:, cols]
        k = k_ref[:, cols]
        v = v_ref[:, cols]
        s = lax.dot_general(q, k, NT_DIMS, preferred_element_type=F32) * decay_scr[h]
        inner = jnp.dot(s.astype(BF16), v, preferred_element_type=F32)
        state = state_scr[h]
        cross = jnp.dot(q, state.astype(BF16), preferred_element_type=F32) * xi_scr[h]
        o = inner + cross
        vz = (v.astype(F32) * zeta_scr[h]).astype(BF16)
        chunk_decay = jnp.exp(jnp.full((1, 1), lg * chunk, F32))
        state_scr[h] = chunk_decay * state + lax.dot_general(
            k, vz, TN_DIMS, preferred_element_type=F32)
        ms = jnp.mean(o * o, axis=-1, keepdims=True)
        o_ref[:, cols] = (o * lax.rsqrt(ms + NORM_EPS)
                          * g_ref[:, cols].astype(F32)).astype(BF16)


def _retention(log_gamma, proj, meta_k, meta_v, batch, seq, chunk):
    nc = seq // chunk
    tok = lambda col: pl.BlockSpec((chunk, REGION), lambda b, c, lg: (b * nc + c, col))
    meta = pl.BlockSpec((META_PAD, REGION), lambda b, c, lg: (0, 0))
    grid_spec = pltpu.PrefetchScalarGridSpec(
        num_scalar_prefetch=1,
        grid=(batch, nc),
        in_specs=[tok(COL_R_Q), tok(COL_R_K), tok(COL_R_V), tok(COL_R_G), meta, meta],
        out_specs=pl.BlockSpec((chunk, REGION), lambda b, c, lg: (b * nc + c, 0)),
        scratch_shapes=[
            pltpu.VMEM((RET_HEADS, RET_QK_DIM, RET_V_DIM), F32),
            pltpu.VMEM((RET_HEADS, chunk, chunk), F32),
            pltpu.VMEM((RET_HEADS, chunk, RET_V_DIM), F32),
            pltpu.VMEM((RET_HEADS, chunk, RET_V_DIM), F32),
        ],
    )
    return pl.pallas_call(
        functools.partial(_ret_kernel, chunk=chunk),
        grid_spec=grid_spec,
        out_shape=jax.ShapeDtypeStruct((batch * seq, RET_HEADS * RET_V_DIM), BF16),
        compiler_params=pltpu.CompilerParams(
            dimension_semantics=("arbitrary", "arbitrary"),
            vmem_limit_bytes=VMEM_LIMIT),
        name="retention",
    )(log_gamma, proj, proj, proj, proj, meta_k, meta_v)


def _merge_kernel(oa_ref, or_ref, ga0_ref, ga1_ref, gr0_ref, gr1_ref, x_ref,
                  wpa_ref, wpr_ref, wo_ref, g2_ref, h_ref, hn_ref):
    oa = oa_ref[...]
    orr = or_ref[...]
    h = x_ref[...]
    for n, (ga_ref, gr_ref) in enumerate(((ga0_ref, gr0_ref), (ga1_ref, gr1_ref))):
        cols = slice(n * REGION, (n + 1) * REGION)
        ya = jnp.dot(oa, wpa_ref[:, cols], preferred_element_type=F32)
        yr = jnp.dot(orr, wpr_ref[:, cols], preferred_element_type=F32)
        merged = ga_ref[...].astype(F32) * ya + gr_ref[...].astype(F32) * yr
        h = h + jnp.dot(merged.astype(BF16), wo_ref[cols, :], preferred_element_type=F32)
    h_ref[...] = h
    ms = jnp.mean(h * h, axis=-1, keepdims=True)
    hn_ref[...] = (h * lax.rsqrt(ms + NORM_EPS) * g2_ref[...]).astype(BF16)


def _merge(oa, orr, proj, x2d, wpa, wpr, wo, g2, tm):
    m = x2d.shape[0]
    row = lambda width: pl.BlockSpec((tm, width), lambda i: (i, 0))
    gate = lambda col: pl.BlockSpec((tm, REGION), lambda i: (i, col))
    whole = lambda shape: pl.BlockSpec(shape, lambda i: (0, 0), pipeline_mode=pl.Buffered(1))
    return pl.pallas_call(
        _merge_kernel,
        grid=(m // tm,),
        in_specs=[
            row(REGION), row(REGION),
            gate(COL_G_A), gate(COL_G_A + 1), gate(COL_G_R), gate(COL_G_R + 1),
            row(D_MODEL),
            whole((REGION, D_MODEL)), whole((REGION, D_MODEL)), whole((D_MODEL, D_MODEL)),
            whole((1, D_MODEL)),
        ],
        out_specs=[row(D_MODEL), row(D_MODEL)],
        out_shape=[jax.ShapeDtypeStruct((m, D_MODEL), F32),
                   jax.ShapeDtypeStruct((m, D_MODEL), BF16)],
        compiler_params=pltpu.CompilerParams(
            dimension_semantics=("arbitrary",),
            vmem_limit_bytes=VMEM_LIMIT),
        name="merge",
    )(oa, orr, proj, proj, proj, proj, x2d, wpa, wpr, wo, g2)


def _mlp_kernel(hn_ref, wup_ref, wdown_ref, h_ref, gf_ref, o_ref):
    f = pl.program_id(1)
    last = pl.num_programs(1) - 1

    def ffn_chunk():
        u = jnp.dot(hn_ref[...], wup_ref[...], preferred_element_type=F32)
        a = jnp.square(jnp.maximum(u, 0.0)).astype(BF16)
        return jnp.dot(a, wdown_ref[...], preferred_element_type=F32)

    @pl.when(f == 0)
    def _():
        o_ref[...] = h_ref[...] + ffn_chunk()

    @pl.when((f > 0) & (f < last))
    def _():
        o_ref[...] += ffn_chunk()

    @pl.when(f == last)
    def _():
        y = o_ref[...] + ffn_chunk()
        ms = jnp.mean(y * y, axis=-1, keepdims=True)
        o_ref[...] = y * lax.rsqrt(ms + NORM_EPS) * gf_ref[...]


def _mlp(hn, h1, wup, wdown, gf, tm, tf):
    m = hn.shape[0]
    return pl.pallas_call(
        _mlp_kernel,
        grid=(m // tm, D_FF // tf),
        in_specs=[
            pl.BlockSpec((tm, D_MODEL), lambda i, f: (i, 0)),
            pl.BlockSpec((D_MODEL, tf), lambda i, f: (0, f)),
            pl.BlockSpec((tf, D_MODEL), lambda i, f: (f, 0)),
            pl.BlockSpec((tm, D_MODEL), lambda i, f: (i, 0)),
            pl.BlockSpec((1, D_MODEL), lambda i, f: (0, 0)),
        ],
        out_specs=pl.BlockSpec((tm, D_MODEL), lambda i, f: (i, 0)),
        out_shape=jax.ShapeDtypeStruct((m, D_MODEL), F32),
        compiler_params=pltpu.CompilerParams(
            dimension_semantics=("arbitrary", "arbitrary"),
            vmem_limit_bytes=VMEM_LIMIT),
        name="mlp",
    )(hn, wup, wdown, h1, gf)


def _rope_table(pos):
    def inv_freq(half):
        return ROPE_THETA ** (-jnp.arange(half, dtype=F32) / half)

    inv_a, inv_r = inv_freq(DA_HEAD_DIM // 2), inv_freq(RET_QK_DIM // 2)
    ang = pos.astype(F32)[:, None] * jnp.concatenate([inv_a, inv_a, inv_r])[None, :]
    sign = jnp.concatenate([-jnp.ones_like(inv_a), jnp.ones_like(inv_a), jnp.ones_like(inv_r)])
    return jnp.concatenate([jnp.cos(ang), jnp.sin(ang) * sign[None, :]], axis=-1)


def kernel(x, meta_tokens, norm1_g, w_in, lam_q1, lam_k1, lam_q2, lam_k2, da_subln_g,
           w_pa, w_pr, w_o, norm2_g, w_up, w_down, normf_g):
    batch, seq, d = x.shape
    x2d = x.reshape(batch * seq, d)
    w_in_b = w_in[0].astype(BF16)
    g1 = norm1_g[0].reshape(1, d)

    tab_real = _rope_table(N_META + jnp.arange(seq))
    tab_meta = _rope_table(jnp.arange(N_META))

    main_tiles = tuple((r, r + 1 if r + 1 < N_REGIONS else None) for r in range(0, N_REGIONS, 2))
    proj, vt = _inproj(x2d, g1, w_in_b, tab_real, tm=INPROJ_ROWS, tiles=main_tiles, emit_vt=True)
    meta_tiles = ((COL_DA_K, COL_DA_V), (COL_R_K, COL_R_V))
    proj_meta, = _inproj(meta_tokens.astype(F32), g1, w_in_b, tab_meta, tm=N_META,
                         tiles=meta_tiles, emit_vt=False)
    meta_regions = tuple(r for tile in meta_tiles for r in tile)
    pad = ((0, META_PAD - N_META), (0, 0))

    def sl(col):
        at = meta_regions.index(col) * REGION
        return jnp.pad(proj_meta[:, at:at + REGION], pad)

    row = lambda a: a[0].reshape(1, -1)
    attention = functools.partial(
        _diff_attention, proj, vt, sl(COL_DA_K), sl(COL_DA_V).T, row(lam_q1), row(lam_k1),
        row(lam_q2), row(lam_k2), row(da_subln_g), batch, seq)
    oa, finite, w_pa_b, w_pr_b, w_o_b, w_up_b, w_down_b = attention(
        cast=(w_pa[0], w_pr[0], w_o[0], w_up[0], w_down[0]), running_max=False)
    oa = lax.cond(jnp.min(finite) > 0.5, lambda: oa,
                  lambda: attention(cast=(), running_max=True)[0])

    log_gamma = jnp.log(1.0 - 2.0 ** (-5.0 - jnp.arange(RET_HEADS, dtype=F32)))
    orr = _retention(log_gamma, proj, sl(COL_R_K), sl(COL_R_V), batch, seq, chunk=RET_CHUNK)

    h1, hn = _merge(oa, orr, proj, x2d, w_pa_b, w_pr_b, w_o_b, row(norm2_g), tm=MERGE_ROWS)
    out = _mlp(hn, h1, w_up_b, w_down_b, normf_g.reshape(1, d), tm=MLP_ROWS, tf=MLP_FF_TILE)
    return out.reshape(batch, seq, d)
```

```python
import functools
import math

import jax
import jax.numpy as jnp
from jax import lax
from jax.experimental import pallas as pl
from jax.experimental.pallas import tpu as pltpu

F32 = jnp.float32
BF16 = jnp.bfloat16

D_MODEL = 2048
N_META = 16
ROPE_THETA = 10000.0
NORM_EPS = 1e-6
DA_HEADS = 4
DA_HEAD_DIM = 128
DA_V_DIM = 2 * DA_HEAD_DIM
RET_HEADS = 4
RET_QK_DIM = 256
RET_V_DIM = 256
D_FF = 4 * D_MODEL
LAMBDA_INIT = 0.8 - 0.6 * math.exp(-0.3 * 0)

REGION = 1024
COL_DA_Q, COL_DA_K, COL_DA_V, COL_R_Q, COL_R_K, COL_R_V, COL_R_G, COL_G_A, COL_G_R = (
    0, 1, 2, 3, 4, 5, 6, 7, 9)
N_REGIONS = 11

LANES = 128
META_PAD = 128
ONES_ROWS = 16
NEG_BIG = -1e30
LOG2_E = math.log2(math.e)
ROPE_SPLIT = 64
VMEM_LIMIT = 58 * 1024 * 1024

INPROJ_ROWS = 1024
RET_CHUNK = 512
MERGE_ROWS = 512
MLP_ROWS = 512
MLP_FF_TILE = 2048

NT_DIMS = (((1,), (1,)), ((), ()))
TN_DIMS = (((0,), (0,)), ((), ()))


def _sigmoid(x):
    return 0.5 + 0.5 * jnp.tanh(0.5 * x)


def _pick(t, values):
    r = values[0]
    for idx, v in enumerate(values[1:], 1):
        r = jnp.where(t >= idx, v, r)
    return r


def _inproj_kernel(x_hbm, g_ref, wa_ref, wb_ref, tab_ref, o_ref, *rest, tiles, emit_vt, tm):
    vt_ref = rest[0] if emit_vt else None
    xbuf, xn_ref, sem = rest[-3:]
    i = pl.program_id(0)
    t = pl.program_id(1)

    def x_copy(row_tile):
        rows = pl.ds(pl.multiple_of(row_tile * tm, tm), tm)
        return pltpu.make_async_copy(x_hbm.at[rows, :], xbuf, sem)

    @pl.when((t == 0) & (i == 0))
    def _():
        x_copy(0).start()

    @pl.when(t == 0)
    def _():
        x_copy(i).wait()
        x = xbuf[...]
        ms = jnp.mean(x * x, axis=-1, keepdims=True)
        xn_ref[...] = (x * lax.rsqrt(ms + NORM_EPS) * g_ref[...]).astype(BF16)

    @pl.when((t == 1) & (i + 1 < pl.num_programs(0)))
    def _():
        x_copy(i + 1).start()

    def scaled(tab, scale):
        return tab if scale == 1.0 else tab * scale

    def rope_da(acc, base, scale):
        cos = scaled(tab_ref[:, 0:LANES], scale)
        sin = scaled(tab_ref[:, 2 * LANES:3 * LANES], scale)
        for c in range(REGION // LANES):
            xc = acc[:, c * LANES:(c + 1) * LANES]
            r = xc * cos + pltpu.roll(xc, LANES // 2, 1) * sin
            o_ref[:, base + c * LANES:base + (c + 1) * LANES] = r.astype(BF16)

    def rope_ret(acc, base, scale):
        cos = scaled(tab_ref[:, LANES:2 * LANES], scale)
        sin = scaled(tab_ref[:, 3 * LANES:4 * LANES], scale)
        for h in range(RET_HEADS):
            lo = h * RET_QK_DIM
            x1 = acc[:, lo:lo + LANES]
            x2 = acc[:, lo + LANES:lo + 2 * LANES]
            o_ref[:, base + lo:base + lo + LANES] = (x1 * cos - x2 * sin).astype(BF16)
            o_ref[:, base + lo + LANES:base + lo + 2 * LANES] = (x2 * cos + x1 * sin).astype(BF16)

    def epilogue(region, acc, base):
        out = slice(base, base + REGION)
        if region == COL_DA_Q:
            rope_da(acc, base, DA_HEAD_DIM ** -0.5 * LOG2_E)
        elif region == COL_DA_K:
            rope_da(acc, base, 1.0)
        elif region == COL_R_Q:
            rope_ret(acc, base, 1.0)
        elif region == COL_R_K:
            rope_ret(acc, base, RET_QK_DIM ** -0.5)
        elif region == COL_DA_V:
            o_ref[:, out] = acc.astype(BF16)
            if emit_vt:
                vt_ref[0] = acc.T.astype(BF16)
        elif region == COL_R_V:
            o_ref[:, out] = acc.astype(BF16)
        elif region == COL_R_G:
            o_ref[:, out] = (acc * _sigmoid(acc)).astype(BF16)
        else:
            o_ref[:, out] = _sigmoid(acc).astype(BF16)

    for idx, tile in enumerate(tiles):
        @pl.when(t == idx)
        def _(tile=tile):
            for half, (region, w_ref) in enumerate(zip(tile, (wa_ref, wb_ref))):
                base = half * REGION
                if region is None:
                    o_ref[:, base:base + REGION] = jnp.zeros((tm, REGION), BF16)
                else:
                    acc = jnp.dot(xn_ref[...], w_ref[...], preferred_element_type=F32)
                    epilogue(region, acc, base)


def _inproj(x2d, g, w_bf16, tab, tm, tiles, emit_vt):
    m = x2d.shape[0]
    n_rows = m // tm
    n_tab = tab.shape[0] // tm
    assert len(tiles) >= 2
    first = tuple(a for a, _ in tiles)
    second = tuple(b if b is not None else tiles[idx - 1][1] for idx, (_, b) in enumerate(tiles))
    out_specs = [pl.BlockSpec((tm, 2 * REGION), lambda i, t: (i, t))]
    out_shape = [jax.ShapeDtypeStruct((m, len(tiles) * 2 * REGION), BF16)]
    if emit_vt:
        out_specs.append(pl.BlockSpec((1, REGION, tm), lambda i, t: (i, 0, 0)))
        out_shape.append(jax.ShapeDtypeStruct((n_rows, REGION, tm), BF16))
    return pl.pallas_call(
        functools.partial(_inproj_kernel, tiles=tiles, emit_vt=emit_vt, tm=tm),
        grid=(n_rows, len(tiles)),
        in_specs=[
            pl.BlockSpec(memory_space=pl.ANY),
            pl.BlockSpec((1, D_MODEL), lambda i, t: (0, 0)),
            pl.BlockSpec((D_MODEL, REGION), lambda i, t: (0, _pick(t, first))),
            pl.BlockSpec((D_MODEL, REGION), lambda i, t: (0, _pick(t, second))),
            pl.BlockSpec((tm, 4 * LANES), lambda i, t: (i % n_tab, 0)),
        ],
        out_specs=out_specs,
        out_shape=out_shape,
        scratch_shapes=[
            pltpu.VMEM((tm, D_MODEL), F32),
            pltpu.VMEM((tm, D_MODEL), BF16),
            pltpu.SemaphoreType.DMA(()),
        ],
        compiler_params=pltpu.CompilerParams(
            dimension_semantics=("arbitrary", "arbitrary"),
            vmem_limit_bytes=VMEM_LIMIT),
        name="inproj",
    )(x2d, g, w_bf16, w_bf16, tab)


def _attn_kernel(lq1_ref, lk1_ref, lq2_ref, lk2_ref, subg_ref, q_ref, k_ref, vt_ref,
                 km_ref, vmt_ref, *rest, tq, n_q_tiles, n_cast, running_max):
    cast_in = rest[:n_cast]
    o_ref, finite_ref = rest[n_cast:n_cast + 2]
    cast_out = rest[n_cast + 2:2 * n_cast + 2]
    m_scr, l_scr, acc_scr = rest[2 * n_cast + 2:]
    qi = pl.program_id(2)
    half = tq // 2
    every = slice(None)

    for src, dst in zip(cast_in, cast_out):
        dst[...] = src[...].astype(BF16)

    def scores(k_blk, queries):
        out = []
        for c in range(2):
            lanes = slice(c * DA_HEAD_DIM, (c + 1) * DA_HEAD_DIM)
            out.append(lax.dot_general(k_blk[:, lanes], q_ref[queries, lanes], NT_DIMS,
                                       preferred_element_type=F32))
        return out

    def absorb(block_scores, vt_blk, queries, mask, first):
        vt = vt_blk[...]
        vt = jnp.concatenate([vt, jnp.ones((ONES_ROWS, vt.shape[1]), BF16)], axis=0)
        for c, s in enumerate(block_scores):
            if mask is not None:
                s = jnp.where(mask, s, NEG_BIG)
            alpha = None
            if first:
                m_ref = jnp.max(s, axis=0, keepdims=True)
                m_scr[c, :, queries] = m_ref
            elif running_max:
                m_old = m_scr[c, :, queries]
                m_ref = jnp.maximum(m_old, jnp.max(s, axis=0, keepdims=True))
                alpha = jnp.exp2(m_old - m_ref)
                m_scr[c, :, queries] = m_ref
            else:
                m_ref = m_scr[c, :, queries]
            p = jnp.exp2(s - m_ref).astype(BF16)
            pv_ext = jnp.dot(vt, p, preferred_element_type=F32)
            pv = pv_ext[0:DA_V_DIM]
            psum = pv_ext[DA_V_DIM:DA_V_DIM + 1]
            if first:
                l_scr[c, :, queries] = psum
                acc_scr[c, :, queries] = pv
            elif alpha is None:
                l_scr[c, :, queries] += psum
                acc_scr[c, :, queries] += pv
            else:
                l_scr[c, :, queries] = alpha * l_scr[c, :, queries] + psum
                acc_scr[c, :, queries] = alpha * acc_scr[c, :, queries] + pv

    def causal(n_q):
        key = lax.broadcasted_iota(jnp.int32, (half, n_q), 0)
        query = lax.broadcasted_iota(jnp.int32, (half, n_q), 1)
        return key <= query

    def key_blocks(n_full):
        meta_mask = lax.broadcasted_iota(jnp.int32, (META_PAD, tq), 0) < N_META
        blocks = [(km_ref, vmt_ref, every, meta_mask, True)]
        for j in range(n_full):
            blocks.append((k_ref.at[j * tq:(j + 1) * tq, :], vt_ref.at[j], every, None, False))
        diag = n_full * tq
        blocks.append((k_ref.at[diag:diag + half, :], vt_ref.at[n_full, :, 0:half], every,
                       causal(tq), False))
        blocks.append((k_ref.at[diag + half:diag + tq, :], vt_ref.at[n_full, :, half:tq],
                       slice(half, tq), causal(half), False))
        return blocks

    for n_full in range(n_q_tiles):
        @pl.when(qi == n_full)
        def _(n_full=n_full):
            blocks = key_blocks(n_full)
            ahead = scores(blocks[0][0], blocks[0][2])
            for t, (_, vt_blk, queries, mask, first) in enumerate(blocks):
                current = ahead
                if t + 1 < len(blocks):
                    ahead = scores(blocks[t + 1][0], blocks[t + 1][2])
                absorb(current, vt_blk, queries, mask, first)

    lam = (jnp.exp(jnp.sum(lq1_ref[...] * lk1_ref[...], axis=-1, keepdims=True))
           - jnp.exp(jnp.sum(lq2_ref[...] * lk2_ref[...], axis=-1, keepdims=True))
           + LAMBDA_INIT)
    l1, l2 = l_scr[0], l_scr[1]
    o_t = acc_scr[0] * (1.0 / l1) - acc_scr[1] * (lam / l2)
    o = o_t.T
    ms = jnp.mean(o * o, axis=-1, keepdims=True)
    o = o * lax.rsqrt(ms + NORM_EPS) * subg_ref[...] * (1.0 - LAMBDA_INIT)
    o_ref[...] = o.astype(BF16)
    big = jnp.finfo(F32).max
    finite_l = jnp.where((l1 <= big) & (l2 <= big), 1.0, 0.0)
    finite_o = jnp.min(jnp.where(jnp.abs(o_t) <= big, 1.0, 0.0), axis=0, keepdims=True)
    finite = jnp.min(jnp.minimum(finite_l, finite_o), axis=1, keepdims=True)
    finite_ref[...] = jnp.broadcast_to(finite, finite_ref.shape)


def _diff_attention(proj, vt, meta_k, meta_vt, lq1, lk1, lq2, lk2, subg, batch, seq, cast,
                    running_max):
    tq = vt.shape[2]
    nq = seq // tq
    n_steps = batch * DA_HEADS * nq
    step = lambda b, h, i: (b * DA_HEADS + h) * nq + i
    vec = pl.BlockSpec((1, DA_HEAD_DIM), lambda b, h, i: (0, 0))
    per_head = REGION // DA_V_DIM
    cast_specs = [pl.BlockSpec((a.shape[0] // n_steps, a.shape[1]),
                               lambda b, h, i: (step(b, h, i), 0)) for a in cast]
    flag_tile = (8, LANES)
    return pl.pallas_call(
        functools.partial(_attn_kernel, tq=tq, n_q_tiles=nq, n_cast=len(cast),
                          running_max=running_max),
        grid=(batch, DA_HEADS, nq),
        in_specs=[
            vec, vec, vec, vec,
            pl.BlockSpec((1, DA_V_DIM), lambda b, h, i: (0, 0)),
            pl.BlockSpec((tq, DA_V_DIM), lambda b, h, i: (b * nq + i, COL_DA_Q * per_head + h)),
            pl.BlockSpec((seq, DA_V_DIM), lambda b, h, i: (b, COL_DA_K * per_head + h)),
            pl.BlockSpec((nq, DA_V_DIM, tq), lambda b, h, i: (b, h, 0)),
            pl.BlockSpec((META_PAD, DA_V_DIM), lambda b, h, i: (0, h)),
            pl.BlockSpec((DA_V_DIM, META_PAD), lambda b, h, i: (h, 0)),
        ] + cast_specs,
        out_specs=[pl.BlockSpec((tq, DA_V_DIM), lambda b, h, i: (b * nq + i, h)),
                   pl.BlockSpec((1,) + flag_tile, lambda b, h, i: (step(b, h, i), 0, 0))]
        + cast_specs,
        out_shape=[jax.ShapeDtypeStruct((batch * seq, DA_HEADS * DA_V_DIM), BF16),
                   jax.ShapeDtypeStruct((n_steps,) + flag_tile, F32)]
        + [jax.ShapeDtypeStruct(a.shape, BF16) for a in cast],
        scratch_shapes=[
            pltpu.VMEM((2, 1, tq), F32),
            pltpu.VMEM((2, 1, tq), F32),
            pltpu.VMEM((2, DA_V_DIM, tq), F32),
        ],
        compiler_params=pltpu.CompilerParams(
            dimension_semantics=("arbitrary", "arbitrary", "arbitrary"),
            vmem_limit_bytes=VMEM_LIMIT),
        name="diff_attention",
    )(lq1, lk1, lq2, lk2, subg, proj, proj, vt, meta_k, meta_vt, *cast)


def _ret_kernel(lg_ref, q_ref, k_ref, v_ref, g_ref, km_ref, vm_ref, o_ref,
                state_scr, decay_scr, xi_scr, zeta_scr, *, chunk):
    b = pl.program_id(0)
    c = pl.program_id(1)
    heads = [(h, lg_ref[h], slice(h * RET_V_DIM, (h + 1) * RET_V_DIM))
             for h in range(RET_HEADS)]

    @pl.when((b == 0) & (c == 0))
    def _():
        row = lax.broadcasted_iota(jnp.int32, (chunk, chunk), 0)
        col = lax.broadcasted_iota(jnp.int32, (chunk, chunk), 1)
        rel = (row - col).astype(F32)
        idx = lax.broadcasted_iota(jnp.int32, (chunk, RET_V_DIM), 0).astype(F32)
        for h, lg, _ in heads:
            decay_scr[h] = jnp.where(rel >= 0, jnp.exp(lg * jnp.maximum(rel, 0.0)), 0.0)
            xi_scr[h] = jnp.exp(lg * (idx + 1.0))
            zeta_scr[h] = jnp.exp(lg * (chunk - 1.0 - idx))

    @pl.when(c == 0)
    def _():
        midx = lax.broadcasted_iota(jnp.int32, (META_PAD, RET_V_DIM), 0).astype(F32)
        for h, lg, cols in heads:
            mz = jnp.exp(lg * (N_META - 1.0 - midx))
            vz = (vm_ref[:, cols].astype(F32) * mz).astype(BF16)
            state_scr[h] = lax.dot_general(km_ref[:, cols], vz, TN_DIMS,
                                           preferred_element_type=F32)

    for h, lg, cols in heads:
        q = q_ref[:, cols]
        k = k_ref[:, cols]
        v = v_ref[:, cols]
        s = lax.dot_general(q, k, NT_DIMS, preferred_element_type=F32) * decay_scr[h]
        inner = jnp.dot(s.astype(BF16), v, preferred_element_type=F32)
        state = state_scr[h]
        cross = jnp.dot(q, state.astype(BF16), preferred_element_type=F32) * xi_scr[h]
        o = inner + cross
        vz = (v.astype(F32) * zeta_scr[h]).astype(BF16)
        chunk_decay = jnp.exp(jnp.full((1, 1), lg * chunk, F32))
        state_scr[h] = chunk_decay * state + lax.dot_general(
            k, vz, TN_DIMS, preferred_element_type=F32)
        ms = jnp.mean(o * o, axis=-1, keepdims=True)
        o_ref[:, cols] = (o * lax.rsqrt(ms + NORM_EPS)
                          * g_ref[:, cols].astype(F32)).astype(BF16)


def _retention(log_gamma, proj, meta_k, meta_v, batch, seq, chunk):
    nc = seq // chunk
    tok = lambda col: pl.BlockSpec((chunk, REGION), lambda b, c, lg: (b * nc + c, col))
    meta = pl.BlockSpec((META_PAD, REGION), lambda b, c, lg: (0, 0))
    grid_spec = pltpu.PrefetchScalarGridSpec(
        num_scalar_prefetch=1,
        grid=(batch, nc),
        in_specs=[tok(COL_R_Q), tok(COL_R_K), tok(COL_R_V), tok(COL_R_G), meta, meta],
        out_specs=pl.BlockSpec((chunk, REGION), lambda b, c, lg: (b * nc + c, 0)),
        scratch_shapes=[
            pltpu.VMEM((RET_HEADS, RET_QK_DIM, RET_V_DIM), F32),
            pltpu.VMEM((RET_HEADS, chunk, chunk), F32),
            pltpu.VMEM((RET_HEADS, chunk, RET_V_DIM), F32),
            pltpu.VMEM((RET_HEADS, chunk, RET_V_DIM), F32),
        ],
    )
    return pl.pallas_call(
        functools.partial(_ret_kernel, chunk=chunk),
        grid_spec=grid_spec,
        out_shape=jax.ShapeDtypeStruct((batch * seq, RET_HEADS * RET_V_DIM), BF16),
        compiler_params=pltpu.CompilerParams(
            dimension_semantics=("arbitrary", "arbitrary"),
            vmem_limit_bytes=VMEM_LIMIT),
        name="retention",
    )(log_gamma, proj, proj, proj, proj, meta_k, meta_v)


def _merge_kernel(oa_ref, or_ref, ga0_ref, ga1_ref, gr0_ref, gr1_ref, x_ref,
                  wpa_ref, wpr_ref, wo_ref, g2_ref, h_ref, hn_ref):
    oa = oa_ref[...]
    orr = or_ref[...]
    h = x_ref[...]
    for n, (ga_ref, gr_ref) in enumerate(((ga0_ref, gr0_ref), (ga1_ref, gr1_ref))):
        cols = slice(n * REGION, (n + 1) * REGION)
        ya = jnp.dot(oa, wpa_ref[:, cols], preferred_element_type=F32)
        yr = jnp.dot(orr, wpr_ref[:, cols], preferred_element_type=F32)
        merged = ga_ref[...].astype(F32) * ya + gr_ref[...].astype(F32) * yr
        h = h + jnp.dot(merged.astype(BF16), wo_ref[cols, :], preferred_element_type=F32)
    h_ref[...] = h
    ms = jnp.mean(h * h, axis=-1, keepdims=True)
    hn_ref[...] = (h * lax.rsqrt(ms + NORM_EPS) * g2_ref[...]).astype(BF16)


def _merge(oa, orr, proj, x2d, wpa, wpr, wo, g2, tm):
    m = x2d.shape[0]
    row = lambda width: pl.BlockSpec((tm, width), lambda i: (i, 0))
    gate = lambda col: pl.BlockSpec((tm, REGION), lambda i: (i, col))
    whole = lambda shape: pl.BlockSpec(shape, lambda i: (0, 0), pipeline_mode=pl.Buffered(1))
    return pl.pallas_call(
        _merge_kernel,
        grid=(m // tm,),
        in_specs=[
            row(REGION), row(REGION),
            gate(COL_G_A), gate(COL_G_A + 1), gate(COL_G_R), gate(COL_G_R + 1),
            row(D_MODEL),
            whole((REGION, D_MODEL)), whole((REGION, D_MODEL)), whole((D_MODEL, D_MODEL)),
            whole((1, D_MODEL)),
        ],
        out_specs=[row(D_MODEL), row(D_MODEL)],
        out_shape=[jax.ShapeDtypeStruct((m, D_MODEL), F32),
                   jax.ShapeDtypeStruct((m, D_MODEL), BF16)],
        compiler_params=pltpu.CompilerParams(
            dimension_semantics=("arbitrary",),
            vmem_limit_bytes=VMEM_LIMIT),
        name="merge",
    )(oa, orr, proj, proj, proj, proj, x2d, wpa, wpr, wo, g2)


def _mlp_kernel(hn_ref, wup_ref, wdown_ref, h_ref, gf_ref, o_ref):
    f = pl.program_id(1)
    last = pl.num_programs(1) - 1

    def ffn_chunk():
        u = jnp.dot(hn_ref[...], wup_ref[...], preferred_element_type=F32)
        a = jnp.square(jnp.maximum(u, 0.0)).astype(BF16)
        return jnp.dot(a, wdown_ref[...], preferred_element_type=F32)

    @pl.when(f == 0)
    def _():
        o_ref[...] = h_ref[...] + ffn_chunk()

    @pl.when((f > 0) & (f < last))
    def _():
        o_ref[...] += ffn_chunk()

    @pl.when(f == last)
    def _():
        y = o_ref[...] + ffn_chunk()
        ms = jnp.mean(y * y, axis=-1, keepdims=True)
        o_ref[...] = y * lax.rsqrt(ms + NORM_EPS) * gf_ref[...]


def _mlp(hn, h1, wup, wdown, gf, tm, tf):
    m = hn.shape[0]
    return pl.pallas_call(
        _mlp_kernel,
        grid=(m // tm, D_FF // tf),
        in_specs=[
            pl.BlockSpec((tm, D_MODEL), lambda i, f: (i, 0)),
            pl.BlockSpec((D_MODEL, tf), lambda i, f: (0, f)),
            pl.BlockSpec((tf, D_MODEL), lambda i, f: (f, 0)),
            pl.BlockSpec((tm, D_MODEL), lambda i, f: (i, 0)),
            pl.BlockSpec((1, D_MODEL), lambda i, f: (0, 0)),
        ],
        out_specs=pl.BlockSpec((tm, D_MODEL), lambda i, f: (i, 0)),
        out_shape=jax.ShapeDtypeStruct((m, D_MODEL), F32),
        compiler_params=pltpu.CompilerParams(
            dimension_semantics=("arbitrary", "arbitrary"),
            vmem_limit_bytes=VMEM_LIMIT),
        name="mlp",
    )(hn, wup, wdown, h1, gf)


def _rope_table(start, count):
    def inv_freq(half):
        return ROPE_THETA ** (-jnp.arange(half, dtype=F32) / half)

    inv_a, inv_r = inv_freq(DA_HEAD_DIM // 2), inv_freq(RET_QK_DIM // 2)
    inv = jnp.concatenate([inv_a, inv_a, inv_r])
    sign = jnp.concatenate([-jnp.ones_like(inv_a), jnp.ones_like(inv_a), jnp.ones_like(inv_r)])

    def cos_sin(pos):
        ang = pos.astype(F32)[:, None] * inv[None, :]
        return jnp.cos(ang), jnp.sin(ang)

    if count <= ROPE_SPLIT or count % ROPE_SPLIT:
        cos, sin = cos_sin(start + jnp.arange(count))
    else:
        cos_c, sin_c = cos_sin(start + ROPE_SPLIT * jnp.arange(count // ROPE_SPLIT))
        cos_f, sin_f = cos_sin(jnp.arange(ROPE_SPLIT))
        cos = (cos_c[:, None] * cos_f[None] - sin_c[:, None] * sin_f[None]).reshape(count, -1)
        sin = (sin_c[:, None] * cos_f[None] + cos_c[:, None] * sin_f[None]).reshape(count, -1)
    return jnp.concatenate([cos, sin * sign[None, :]], axis=-1)


def kernel(x, meta_tokens, norm1_g, w_in, lam_q1, lam_k1, lam_q2, lam_k2, da_subln_g,
           w_pa, w_pr, w_o, norm2_g, w_up, w_down, normf_g):
    batch, seq, d = x.shape
    x2d = x.reshape(batch * seq, d)
    w_in_b = w_in[0].astype(BF16)
    g1 = norm1_g[0].reshape(1, d)

    tab_real = _rope_table(N_META, seq)
    tab_meta = _rope_table(0, N_META)

    main_tiles = tuple((r, r + 1 if r + 1 < N_REGIONS else None) for r in range(0, N_REGIONS, 2))
    proj, vt = _inproj(x2d, g1, w_in_b, tab_real, tm=INPROJ_ROWS, tiles=main_tiles, emit_vt=True)
    meta_tiles = ((COL_DA_K, COL_DA_V), (COL_R_K, COL_R_V))
    proj_meta, = _inproj(meta_tokens.astype(F32), g1, w_in_b, tab_meta, tm=N_META,
                         tiles=meta_tiles, emit_vt=False)
    meta_regions = tuple(r for tile in meta_tiles for r in tile)
    pad = ((0, META_PAD - N_META), (0, 0))

    def sl(col):
        at = meta_regions.index(col) * REGION
        return jnp.pad(proj_meta[:, at:at + REGION], pad)

    row = lambda a: a[0].reshape(1, -1)
    attention = functools.partial(
        _diff_attention, proj, vt, sl(COL_DA_K), sl(COL_DA_V).T, row(lam_q1), row(lam_k1),
        row(lam_q2), row(lam_k2), row(da_subln_g), batch, seq)
    oa, finite, w_pa_b, w_pr_b, w_o_b, w_up_b, w_down_b = attention(
        cast=(w_pa[0], w_pr[0], w_o[0], w_up[0], w_down[0]), running_max=False)
    oa = lax.cond(jnp.min(finite) > 0.5, lambda: oa,
                  lambda: attention(cast=(), running_max=True)[0])

    log_gamma = jnp.log(1.0 - 2.0 ** (-5.0 - jnp.arange(RET_HEADS, dtype=F32)))
    orr = _retention(log_gamma, proj, sl(COL_R_K), sl(COL_R_V), batch, seq, chunk=RET_CHUNK)

    h1, hn = _merge(oa, orr, proj, x2d, w_pa_b, w_pr_b, w_o_b, row(norm2_g), tm=MERGE_ROWS)
    out = _mlp(hn, h1, w_up_b, w_down_b, normf_g.reshape(1, d), tm=MLP_ROWS, tf=MLP_FF_TILE)
    return out.reshape(batch, seq, d)
```

```python
import functools
import math

import jax
import jax.numpy as jnp
from jax import lax
from jax.experimental import pallas as pl
from jax.experimental.pallas import tpu as pltpu

F32 = jnp.float32
BF16 = jnp.bfloat16

D_MODEL = 2048
N_META = 16
ROPE_THETA = 10000.0
NORM_EPS = 1e-6
DA_HEADS = 4
DA_HEAD_DIM = 128
DA_V_DIM = 2 * DA_HEAD_DIM
RET_HEADS = 4
RET_QK_DIM = 256
RET_V_DIM = 256
D_FF = 4 * D_MODEL
LAMBDA_INIT = 0.8 - 0.6 * math.exp(-0.3 * 0)

REGION = 1024
COL_DA_Q, COL_DA_K, COL_DA_V, COL_R_Q, COL_R_K, COL_R_V, COL_R_G, COL_G_A, COL_G_R = (
    0, 1, 2, 3, 4, 5, 6, 7, 9)
N_REGIONS = 11

LANES = 128
META_PAD = 128
ONES_ROWS = 16
NEG_BIG = -1e30
LOG2_E = math.log2(math.e)
ROPE_SPLIT = 64
DIAG_STEPS = 4
VMEM_LIMIT = 58 * 1024 * 1024

INPROJ_ROWS = 1024
RET_CHUNK = 512
MERGE_ROWS = 512
MLP_ROWS = 512
MLP_FF_TILE = 2048

NT_DIMS = (((1,), (1,)), ((), ()))
TN_DIMS = (((0,), (0,)), ((), ()))


def _sigmoid(x):
    return 0.5 + 0.5 * jnp.tanh(0.5 * x)


def _pick(t, values):
    r = values[0]
    for idx, v in enumerate(values[1:], 1):
        r = jnp.where(t >= idx, v, r)
    return r


def _inproj_kernel(x_hbm, g_ref, wa_ref, wb_ref, tab_ref, o_ref, *rest, tiles, emit_vt, tm):
    vt_ref = rest[0] if emit_vt else None
    xbuf, xn_ref, sem = rest[-3:]
    i = pl.program_id(0)
    t = pl.program_id(1)

    def x_copy(row_tile):
        rows = pl.ds(pl.multiple_of(row_tile * tm, tm), tm)
        return pltpu.make_async_copy(x_hbm.at[rows, :], xbuf, sem)

    @pl.when((t == 0) & (i == 0))
    def _():
        x_copy(0).start()

    @pl.when(t == 0)
    def _():
        x_copy(i).wait()
        x = xbuf[...]
        ms = jnp.mean(x * x, axis=-1, keepdims=True)
        xn_ref[...] = (x * lax.rsqrt(ms + NORM_EPS) * g_ref[...]).astype(BF16)

    @pl.when((t == 1) & (i + 1 < pl.num_programs(0)))
    def _():
        x_copy(i + 1).start()

    def scaled(tab, scale):
        return tab if scale == 1.0 else tab * scale

    def rope_da(acc, base, scale):
        cos = scaled(tab_ref[:, 0:LANES], scale)
        sin = scaled(tab_ref[:, 2 * LANES:3 * LANES], scale)
        for c in range(REGION // LANES):
            xc = acc[:, c * LANES:(c + 1) * LANES]
            r = xc * cos + pltpu.roll(xc, LANES // 2, 1) * sin
            o_ref[:, base + c * LANES:base + (c + 1) * LANES] = r.astype(BF16)

    def rope_ret(acc, base, scale):
        cos = scaled(tab_ref[:, LANES:2 * LANES], scale)
        sin = scaled(tab_ref[:, 3 * LANES:4 * LANES], scale)
        for h in range(RET_HEADS):
            lo = h * RET_QK_DIM
            x1 = acc[:, lo:lo + LANES]
            x2 = acc[:, lo + LANES:lo + 2 * LANES]
            o_ref[:, base + lo:base + lo + LANES] = (x1 * cos - x2 * sin).astype(BF16)
            o_ref[:, base + lo + LANES:base + lo + 2 * LANES] = (x2 * cos + x1 * sin).astype(BF16)

    def epilogue(region, acc, base):
        out = slice(base, base + REGION)
        if region == COL_DA_Q:
            rope_da(acc, base, DA_HEAD_DIM ** -0.5 * LOG2_E)
        elif region == COL_DA_K:
            rope_da(acc, base, 1.0)
        elif region == COL_R_Q:
            rope_ret(acc, base, 1.0)
        elif region == COL_R_K:
            rope_ret(acc, base, RET_QK_DIM ** -0.5)
        elif region == COL_DA_V:
            o_ref[:, out] = acc.astype(BF16)
            if emit_vt:
                vt_ref[0] = acc.T.astype(BF16)
        elif region == COL_R_V:
            o_ref[:, out] = acc.astype(BF16)
        elif region == COL_R_G:
            o_ref[:, out] = (acc * _sigmoid(acc)).astype(BF16)
        else:
            o_ref[:, out] = _sigmoid(acc).astype(BF16)

    for idx, tile in enumerate(tiles):
        @pl.when(t == idx)
        def _(tile=tile):
            for half, (region, w_ref) in enumerate(zip(tile, (wa_ref, wb_ref))):
                base = half * REGION
                if region is None:
                    o_ref[:, base:base + REGION] = jnp.zeros((tm, REGION), BF16)
                else:
                    acc = jnp.dot(xn_ref[...], w_ref[...], preferred_element_type=F32)
                    epilogue(region, acc, base)


def _inproj(x2d, g, w_bf16, tab, tm, tiles, emit_vt):
    m = x2d.shape[0]
    n_rows = m // tm
    n_tab = tab.shape[0] // tm
    assert len(tiles) >= 2
    first = tuple(a for a, _ in tiles)
    second = tuple(b if b is not None else tiles[idx - 1][1] for idx, (_, b) in enumerate(tiles))
    out_specs = [pl.BlockSpec((tm, 2 * REGION), lambda i, t: (i, t))]
    out_shape = [jax.ShapeDtypeStruct((m, len(tiles) * 2 * REGION), BF16)]
    if emit_vt:
        out_specs.append(pl.BlockSpec((1, REGION, tm), lambda i, t: (i, 0, 0)))
        out_shape.append(jax.ShapeDtypeStruct((n_rows, REGION, tm), BF16))
    return pl.pallas_call(
        functools.partial(_inproj_kernel, tiles=tiles, emit_vt=emit_vt, tm=tm),
        grid=(n_rows, len(tiles)),
        in_specs=[
            pl.BlockSpec(memory_space=pl.ANY),
            pl.BlockSpec((1, D_MODEL), lambda i, t: (0, 0)),
            pl.BlockSpec((D_MODEL, REGION), lambda i, t: (0, _pick(t, first))),
            pl.BlockSpec((D_MODEL, REGION), lambda i, t: (0, _pick(t, second))),
            pl.BlockSpec((tm, 4 * LANES), lambda i, t: (i % n_tab, 0)),
        ],
        out_specs=out_specs,
        out_shape=out_shape,
        scratch_shapes=[
            pltpu.VMEM((tm, D_MODEL), F32),
            pltpu.VMEM((tm, D_MODEL), BF16),
            pltpu.SemaphoreType.DMA(()),
        ],
        compiler_params=pltpu.CompilerParams(
            dimension_semantics=("arbitrary", "arbitrary"),
            vmem_limit_bytes=VMEM_LIMIT),
        name="inproj",
    )(x2d, g, w_bf16, w_bf16, tab)


def _attn_kernel(lq1_ref, lk1_ref, lq2_ref, lk2_ref, subg_ref, q_ref, k_ref, vt_ref,
                 km_ref, vmt_ref, *rest, tq, n_q_tiles, n_cast, running_max):
    cast_in = rest[:n_cast]
    o_ref, finite_ref = rest[n_cast:n_cast + 2]
    cast_out = rest[n_cast + 2:2 * n_cast + 2]
    m_scr, l_scr, acc_scr = rest[2 * n_cast + 2:]
    qi = pl.program_id(2)
    every = slice(None)

    for src, dst in zip(cast_in, cast_out):
        dst[...] = src[...].astype(BF16)

    def scores(k_blk, queries):
        out = []
        for c in range(2):
            lanes = slice(c * DA_HEAD_DIM, (c + 1) * DA_HEAD_DIM)
            out.append(lax.dot_general(k_blk[:, lanes], q_ref[queries, lanes], NT_DIMS,
                                       preferred_element_type=F32))
        return out

    def absorb(block_scores, vt_blk, queries, mask, first):
        vt = vt_blk[...]
        vt = jnp.concatenate([vt, jnp.ones((ONES_ROWS, vt.shape[1]), BF16)], axis=0)
        for c, s in enumerate(block_scores):
            if mask is not None:
                s = jnp.where(mask, s, NEG_BIG)
            alpha = None
            if first:
                m_ref = jnp.max(s, axis=0, keepdims=True)
                m_scr[c, :, queries] = m_ref
            elif running_max:
                m_old = m_scr[c, :, queries]
                m_ref = jnp.maximum(m_old, jnp.max(s, axis=0, keepdims=True))
                alpha = jnp.exp2(m_old - m_ref)
                m_scr[c, :, queries] = m_ref
            else:
                m_ref = m_scr[c, :, queries]
            p = jnp.exp2(s - m_ref).astype(BF16)
            pv_ext = jnp.dot(vt, p, preferred_element_type=F32)
            pv = pv_ext[0:DA_V_DIM]
            psum = pv_ext[DA_V_DIM:DA_V_DIM + 1]
            if first:
                l_scr[c, :, queries] = psum
                acc_scr[c, :, queries] = pv
            elif alpha is None:
                l_scr[c, :, queries] += psum
                acc_scr[c, :, queries] += pv
            else:
                l_scr[c, :, queries] = alpha * l_scr[c, :, queries] + psum
                acc_scr[c, :, queries] = alpha * acc_scr[c, :, queries] + pv

    def causal(n_keys, n_q):
        key = lax.broadcasted_iota(jnp.int32, (n_keys, n_q), 0)
        query = lax.broadcasted_iota(jnp.int32, (n_keys, n_q), 1)
        return key <= query

    def key_blocks(n_full):
        meta_mask = lax.broadcasted_iota(jnp.int32, (META_PAD, tq), 0) < N_META
        blocks = [(km_ref, vmt_ref, every, meta_mask, True)]
        for j in range(n_full):
            blocks.append((k_ref.at[j * tq:(j + 1) * tq, :], vt_ref.at[j], every, None, False))
        diag = n_full * tq
        width = tq // DIAG_STEPS
        for d in range(DIAG_STEPS):
            lo = d * width
            blocks.append((k_ref.at[diag + lo:diag + lo + width, :],
                           vt_ref.at[n_full, :, lo:lo + width], slice(lo, tq),
                           causal(width, tq - lo), False))
        return blocks

    for n_full in range(n_q_tiles):
        @pl.when(qi == n_full)
        def _(n_full=n_full):
            blocks = key_blocks(n_full)
            ahead = scores(blocks[0][0], blocks[0][2])
            for t, (_, vt_blk, queries, mask, first) in enumerate(blocks):
                current = ahead
                if t + 1 < len(blocks):
                    ahead = scores(blocks[t + 1][0], blocks[t + 1][2])
                absorb(current, vt_blk, queries, mask, first)

    lam = (jnp.exp(jnp.sum(lq1_ref[...] * lk1_ref[...], axis=-1, keepdims=True))
           - jnp.exp(jnp.sum(lq2_ref[...] * lk2_ref[...], axis=-1, keepdims=True))
           + LAMBDA_INIT)
    l1, l2 = l_scr[0], l_scr[1]
    o_t = acc_scr[0] * (1.0 / l1) - acc_scr[1] * (lam / l2)
    o = o_t.T
    ms = jnp.mean(o * o, axis=-1, keepdims=True)
    o = o * lax.rsqrt(ms + NORM_EPS) * subg_ref[...] * (1.0 - LAMBDA_INIT)
    o_ref[...] = o.astype(BF16)
    big = jnp.finfo(F32).max
    finite_l = jnp.where((l1 <= big) & (l2 <= big), 1.0, 0.0)
    finite_o = jnp.min(jnp.where(jnp.abs(o_t) <= big, 1.0, 0.0), axis=0, keepdims=True)
    finite = jnp.min(jnp.minimum(finite_l, finite_o), axis=1, keepdims=True)
    finite_ref[...] = jnp.broadcast_to(finite, finite_ref.shape)


def _diff_attention(proj, vt, meta_k, meta_vt, lq1, lk1, lq2, lk2, subg, batch, seq, cast,
                    running_max):
    tq = vt.shape[2]
    nq = seq // tq
    n_steps = batch * DA_HEADS * nq
    step = lambda b, h, i: (b * DA_HEADS + h) * nq + i
    vec = pl.BlockSpec((1, DA_HEAD_DIM), lambda b, h, i: (0, 0))
    per_head = REGION // DA_V_DIM
    cast_specs = [pl.BlockSpec((a.shape[0] // n_steps, a.shape[1]),
                               lambda b, h, i: (step(b, h, i), 0)) for a in cast]
    flag_tile = (8, LANES)
    return pl.pallas_call(
        functools.partial(_attn_kernel, tq=tq, n_q_tiles=nq, n_cast=len(cast),
                          running_max=running_max),
        grid=(batch, DA_HEADS, nq),
        in_specs=[
            vec, vec, vec, vec,
            pl.BlockSpec((1, DA_V_DIM), lambda b, h, i: (0, 0)),
            pl.BlockSpec((tq, DA_V_DIM), lambda b, h, i: (b * nq + i, COL_DA_Q * per_head + h)),
            pl.BlockSpec((seq, DA_V_DIM), lambda b, h, i: (b, COL_DA_K * per_head + h)),
            pl.BlockSpec((nq, DA_V_DIM, tq), lambda b, h, i: (b, h, 0)),
            pl.BlockSpec((META_PAD, DA_V_DIM), lambda b, h, i: (0, h)),
            pl.BlockSpec((DA_V_DIM, META_PAD), lambda b, h, i: (h, 0)),
        ] + cast_specs,
        out_specs=[pl.BlockSpec((tq, DA_V_DIM), lambda b, h, i: (b * nq + i, h)),
                   pl.BlockSpec((1,) + flag_tile, lambda b, h, i: (step(b, h, i), 0, 0))]
        + cast_specs,
        out_shape=[jax.ShapeDtypeStruct((batch * seq, DA_HEADS * DA_V_DIM), BF16),
                   jax.ShapeDtypeStruct((n_steps,) + flag_tile, F32)]
        + [jax.ShapeDtypeStruct(a.shape, BF16) for a in cast],
        scratch_shapes=[
            pltpu.VMEM((2, 1, tq), F32),
            pltpu.VMEM((2, 1, tq), F32),
            pltpu.VMEM((2, DA_V_DIM, tq), F32),
        ],
        compiler_params=pltpu.CompilerParams(
            dimension_semantics=("arbitrary", "arbitrary", "arbitrary"),
            vmem_limit_bytes=VMEM_LIMIT),
        name="diff_attention",
    )(lq1, lk1, lq2, lk2, subg, proj, proj, vt, meta_k, meta_vt, *cast)


def _ret_kernel(lg_ref, q_ref, k_ref, v_ref, g_ref, km_ref, vm_ref, o_ref,
                state_scr, decay_scr, xi_scr, zeta_scr, *, chunk):
    b = pl.program_id(0)
    c = pl.program_id(1)
    heads = [(h, lg_ref[h], slice(h * RET_V_DIM, (h + 1) * RET_V_DIM))
             for h in range(RET_HEADS)]

    @pl.when((b == 0) & (c == 0))
    def _():
        row = lax.broadcasted_iota(jnp.int32, (chunk, chunk), 0)
        col = lax.broadcasted_iota(jnp.int32, (chunk, chunk), 1)
        rel = (row - col).astype(F32)
        idx = lax.broadcasted_iota(jnp.int32, (chunk, RET_V_DIM), 0).astype(F32)
        for h, lg, _ in heads:
            decay_scr[h] = jnp.where(rel >= 0, jnp.exp(lg * jnp.maximum(rel, 0.0)), 0.0)
            xi_scr[h] = jnp.exp(lg * (idx + 1.0))
            zeta_scr[h] = jnp.exp(lg * (chunk - 1.0 - idx))

    @pl.when(c == 0)
    def _():
        midx = lax.broadcasted_iota(jnp.int32, (META_PAD, RET_V_DIM), 0).astype(F32)
        for h, lg, cols in heads:
            mz = jnp.exp(lg * (N_META - 1.0 - midx))
            vz = (vm_ref[:, cols].astype(F32) * mz).astype(BF16)
            state_scr[h] = lax.dot_general(km_ref[:, cols], vz, TN_DIMS,
                                           preferred_element_type=F32)

    for h, lg, cols in heads:
        q = q_ref[:, cols]
        k = k_ref[:, cols]
        v = v_ref[:, cols]
        s = lax.dot_general(q, k, NT_DIMS, preferred_element_type=F32) * decay_scr[h]
        inner = jnp.dot(s.astype(BF16), v, preferred_element_type=F32)
        state = state_scr[h]
        cross = jnp.dot(q, state.astype(BF16), preferred_element_type=F32) * xi_scr[h]
        o = inner + cross
        vz = (v.astype(F32) * zeta_scr[h]).astype(BF16)
        chunk_decay = jnp.exp(jnp.full((1, 1), lg * chunk, F32))
        state_scr[h] = chunk_decay * state + lax.dot_general(
            k, vz, TN_DIMS, preferred_element_type=F32)
        ms = jnp.mean(o * o, axis=-1, keepdims=True)
        o_ref[:, cols] = (o * lax.rsqrt(ms + NORM_EPS)
                          * g_ref[:, cols].astype(F32)).astype(BF16)


def _retention(log_gamma, proj, meta_k, meta_v, batch, seq, chunk):
    nc = seq // chunk
    tok = lambda col: pl.BlockSpec((chunk, REGION), lambda b, c, lg: (b * nc + c, col))
    meta = pl.BlockSpec((META_PAD, REGION), lambda b, c, lg: (0, 0))
    grid_spec = pltpu.PrefetchScalarGridSpec(
        num_scalar_prefetch=1,
        grid=(batch, nc),
        in_specs=[tok(COL_R_Q), tok(COL_R_K), tok(COL_R_V), tok(COL_R_G), meta, meta],
        out_specs=pl.BlockSpec((chunk, REGION), lambda b, c, lg: (b * nc + c, 0)),
        scratch_shapes=[
            pltpu.VMEM((RET_HEADS, RET_QK_DIM, RET_V_DIM), F32),
            pltpu.VMEM((RET_HEADS, chunk, chunk), F32),
            pltpu.VMEM((RET_HEADS, chunk, RET_V_DIM), F32),
            pltpu.VMEM((RET_HEADS, chunk, RET_V_DIM), F32),
        ],
    )
    return pl.pallas_call(
        functools.partial(_ret_kernel, chunk=chunk),
        grid_spec=grid_spec,
        out_shape=jax.ShapeDtypeStruct((batch * seq, RET_HEADS * RET_V_DIM), BF16),
        compiler_params=pltpu.CompilerParams(
            dimension_semantics=("arbitrary", "arbitrary"),
            vmem_limit_bytes=VMEM_LIMIT),
        name="retention",
    )(log_gamma, proj, proj, proj, proj, meta_k, meta_v)


def _merge_kernel(oa_ref, or_ref, ga0_ref, ga1_ref, gr0_ref, gr1_ref, x_ref,
                  wpa_ref, wpr_ref, wo_ref, g2_ref, h_ref, hn_ref):
    oa = oa_ref[...]
    orr = or_ref[...]
    h = x_ref[...]
    for n, (ga_ref, gr_ref) in enumerate(((ga0_ref, gr0_ref), (ga1_ref, gr1_ref))):
        cols = slice(n * REGION, (n + 1) * REGION)
        ya = jnp.dot(oa, wpa_ref[:, cols], preferred_element_type=F32)
        yr = jnp.dot(orr, wpr_ref[:, cols], preferred_element_type=F32)
        merged = ga_ref[...].astype(F32) * ya + gr_ref[...].astype(F32) * yr
        h = h + jnp.dot(merged.astype(BF16), wo_ref[cols, :], preferred_element_type=F32)
    h_ref[...] = h
    ms = jnp.mean(h * h, axis=-1, keepdims=True)
    hn_ref[...] = (h * lax.rsqrt(ms + NORM_EPS) * g2_ref[...]).astype(BF16)


def _merge(oa, orr, proj, x2d, wpa, wpr, wo, g2, tm):
    m = x2d.shape[0]
    row = lambda width: pl.BlockSpec((tm, width), lambda i: (i, 0))
    gate = lambda col: pl.BlockSpec((tm, REGION), lambda i: (i, col))
    whole = lambda shape: pl.BlockSpec(shape, lambda i: (0, 0), pipeline_mode=pl.Buffered(1))
    return pl.pallas_call(
        _merge_kernel,
        grid=(m // tm,),
        in_specs=[
            row(REGION), row(REGION),
            gate(COL_G_A), gate(COL_G_A + 1), gate(COL_G_R), gate(COL_G_R + 1),
            row(D_MODEL),
            whole((REGION, D_MODEL)), whole((REGION, D_MODEL)), whole((D_MODEL, D_MODEL)),
            whole((1, D_MODEL)),
        ],
        out_specs=[row(D_MODEL), row(D_MODEL)],
        out_shape=[jax.ShapeDtypeStruct((m, D_MODEL), F32),
                   jax.ShapeDtypeStruct((m, D_MODEL), BF16)],
        compiler_params=pltpu.CompilerParams(
            dimension_semantics=("arbitrary",),
            vmem_limit_bytes=VMEM_LIMIT),
        name="merge",
    )(oa, orr, proj, proj, proj, proj, x2d, wpa, wpr, wo, g2)


def _mlp_kernel(hn_ref, wup_ref, wdown_ref, h_ref, gf_ref, o_ref):
    f = pl.program_id(1)
    last = pl.num_programs(1) - 1

    def ffn_chunk():
        u = jnp.dot(hn_ref[...], wup_ref[...], preferred_element_type=F32)
        a = jnp.square(jnp.maximum(u, 0.0)).astype(BF16)
        return jnp.dot(a, wdown_ref[...], preferred_element_type=F32)

    @pl.when(f == 0)
    def _():
        o_ref[...] = h_ref[...] + ffn_chunk()

    @pl.when((f > 0) & (f < last))
    def _():
        o_ref[...] += ffn_chunk()

    @pl.when(f == last)
    def _():
        y = o_ref[...] + ffn_chunk()
        ms = jnp.mean(y * y, axis=-1, keepdims=True)
        o_ref[...] = y * lax.rsqrt(ms + NORM_EPS) * gf_ref[...]


def _mlp(hn, h1, wup, wdown, gf, tm, tf):
    m = hn.shape[0]
    return pl.pallas_call(
        _mlp_kernel,
        grid=(m // tm, D_FF // tf),
        in_specs=[
            pl.BlockSpec((tm, D_MODEL), lambda i, f: (i, 0)),
            pl.BlockSpec((D_MODEL, tf), lambda i, f: (0, f)),
            pl.BlockSpec((tf, D_MODEL), lambda i, f: (f, 0)),
            pl.BlockSpec((tm, D_MODEL), lambda i, f: (i, 0)),
            pl.BlockSpec((1, D_MODEL), lambda i, f: (0, 0)),
        ],
        out_specs=pl.BlockSpec((tm, D_MODEL), lambda i, f: (i, 0)),
        out_shape=jax.ShapeDtypeStruct((m, D_MODEL), F32),
        compiler_params=pltpu.CompilerParams(
            dimension_semantics=("arbitrary", "arbitrary"),
            vmem_limit_bytes=VMEM_LIMIT),
        name="mlp",
    )(hn, wup, wdown, h1, gf)


def _rope_table(start, count):
    def inv_freq(half):
        return ROPE_THETA ** (-jnp.arange(half, dtype=F32) / half)

    inv_a, inv_r = inv_freq(DA_HEAD_DIM // 2), inv_freq(RET_QK_DIM // 2)
    inv = jnp.concatenate([inv_a, inv_a, inv_r])
    sign = jnp.concatenate([-jnp.ones_like(inv_a), jnp.ones_like(inv_a), jnp.ones_like(inv_r)])

    def cos_sin(pos):
        ang = pos.astype(F32)[:, None] * inv[None, :]
        return jnp.cos(ang), jnp.sin(ang)

    if count <= ROPE_SPLIT or count % ROPE_SPLIT:
        cos, sin = cos_sin(start + jnp.arange(count))
    else:
        cos_c, sin_c = cos_sin(start + ROPE_SPLIT * jnp.arange(count // ROPE_SPLIT))
        cos_f, sin_f = cos_sin(jnp.arange(ROPE_SPLIT))
        cos = (cos_c[:, None] * cos_f[None] - sin_c[:, None] * sin_f[None]).reshape(count, -1)
        sin = (sin_c[:, None] * cos_f[None] + cos_c[:, None] * sin_f[None]).reshape(count, -1)
    return jnp.concatenate([cos, sin * sign[None, :]], axis=-1)


def kernel(x, meta_tokens, norm1_g, w_in, lam_q1, lam_k1, lam_q2, lam_k2, da_subln_g,
           w_pa, w_pr, w_o, norm2_g, w_up, w_down, normf_g):
    batch, seq, d = x.shape
    x2d = x.reshape(batch * seq, d)
    w_in_b = w_in[0].astype(BF16)
    g1 = norm1_g[0].reshape(1, d)

    tab_real = _rope_table(N_META, seq)
    tab_meta = _rope_table(0, N_META)

    main_tiles = tuple((r, r + 1 if r + 1 < N_REGIONS else None) for r in range(0, N_REGIONS, 2))
    proj, vt = _inproj(x2d, g1, w_in_b, tab_real, tm=INPROJ_ROWS, tiles=main_tiles, emit_vt=True)
    meta_tiles = ((COL_DA_K, COL_DA_V), (COL_R_K, COL_R_V))
    proj_meta, = _inproj(meta_tokens.astype(F32), g1, w_in_b, tab_meta, tm=N_META,
                         tiles=meta_tiles, emit_vt=False)
    meta_regions = tuple(r for tile in meta_tiles for r in tile)
    pad = ((0, META_PAD - N_META), (0, 0))

    def sl(col):
        at = meta_regions.index(col) * REGION
        return jnp.pad(proj_meta[:, at:at + REGION], pad)

    row = lambda a: a[0].reshape(1, -1)
    attention = functools.partial(
        _diff_attention, proj, vt, sl(COL_DA_K), sl(COL_DA_V).T, row(lam_q1), row(lam_k1),
        row(lam_q2), row(lam_k2), row(da_subln_g), batch, seq)
    oa, finite, w_pa_b, w_pr_b, w_o_b, w_up_b, w_down_b = attention(
        cast=(w_pa[0], w_pr[0], w_o[0], w_up[0], w_down[0]), running_max=False)
    oa = lax.cond(jnp.min(finite) > 0.5, lambda: oa,
                  lambda: attention(cast=(), running_max=True)[0])

    log_gamma = jnp.log(1.0 - 2.0 ** (-5.0 - jnp.arange(RET_HEADS, dtype=F32)))
    orr = _retention(log_gamma, proj, sl(COL_R_K), sl(COL_R_V), batch, seq, chunk=RET_CHUNK)

    h1, hn = _merge(oa, orr, proj, x2d, w_pa_b, w_pr_b, w_o_b, row(norm2_g), tm=MERGE_ROWS)
    out = _mlp(hn, h1, w_up_b, w_down_b, normf_g.reshape(1, d), tm=MLP_ROWS, tf=MLP_FF_TILE)
    return out.reshape(batch, seq, d)
```

```python
import functools
import math

import jax
import jax.numpy as jnp
from jax import lax
from jax.experimental import pallas as pl
from jax.experimental.pallas import tpu as pltpu

F32 = jnp.float32
BF16 = jnp.bfloat16

D_MODEL = 2048
N_META = 16
ROPE_THETA = 10000.0
NORM_EPS = 1e-6
DA_HEADS = 4
DA_HEAD_DIM = 128
DA_V_DIM = 2 * DA_HEAD_DIM
RET_HEADS = 4
RET_QK_DIM = 256
RET_V_DIM = 256
D_FF = 4 * D_MODEL
LAMBDA_INIT = 0.8 - 0.6 * math.exp(-0.3 * 0)

REGION = 1024
COL_DA_Q, COL_DA_K, COL_DA_V, COL_R_Q, COL_R_K, COL_R_V, COL_R_G, COL_G_A, COL_G_R = (
    0, 1, 2, 3, 4, 5, 6, 7, 9)
N_REGIONS = 11

LANES = 128
META_PAD = 128
ONES_ROWS = 16
NEG_BIG = -1e30
LOG2_E = math.log2(math.e)
ROPE_SPLIT = 64
DIAG_STEPS = 4
VMEM_LIMIT = 58 * 1024 * 1024

INPROJ_ROWS = 1024
RET_CHUNK = 512
MERGE_ROWS = 512
MLP_ROWS = 512
MLP_FF_TILE = 2048

NT_DIMS = (((1,), (1,)), ((), ()))
TN_DIMS = (((0,), (0,)), ((), ()))


def _sigmoid(x):
    return 0.5 + 0.5 * jnp.tanh(0.5 * x)


def _pick(t, values):
    r = values[0]
    for idx, v in enumerate(values[1:], 1):
        r = jnp.where(t >= idx, v, r)
    return r


def _inproj_kernel(x_hbm, wa_ref, wb_ref, tab_ref, o_ref, *rest, tiles, emit_vt, tm):
    vt_ref = rest[0] if emit_vt else None
    xbuf, xn_ref, sem = rest[-3:]
    i = pl.program_id(0)
    t = pl.program_id(1)

    def x_copy(row_tile):
        rows = pl.ds(pl.multiple_of(row_tile * tm, tm), tm)
        return pltpu.make_async_copy(x_hbm.at[rows, :], xbuf, sem)

    @pl.when((t == 0) & (i == 0))
    def _():
        x_copy(0).start()

    @pl.when(t == 0)
    def _():
        x_copy(i).wait()
        x = xbuf[...]
        ms = jnp.mean(x * x, axis=-1, keepdims=True)
        xn_ref[...] = (x * lax.rsqrt(ms + NORM_EPS)).astype(BF16)

    @pl.when((t == 1) & (i + 1 < pl.num_programs(0)))
    def _():
        x_copy(i + 1).start()

    def scaled(tab, scale):
        return tab if scale == 1.0 else tab * scale

    def rope_da(acc, base, scale):
        cos = scaled(tab_ref[:, 0:LANES], scale)
        sin = scaled(tab_ref[:, 2 * LANES:3 * LANES], scale)
        for c in range(REGION // LANES):
            xc = acc[:, c * LANES:(c + 1) * LANES]
            r = xc * cos + pltpu.roll(xc, LANES // 2, 1) * sin
            o_ref[:, base + c * LANES:base + (c + 1) * LANES] = r.astype(BF16)

    def rope_ret(acc, base, scale):
        cos = scaled(tab_ref[:, LANES:2 * LANES], scale)
        sin = scaled(tab_ref[:, 3 * LANES:4 * LANES], scale)
        for h in range(RET_HEADS):
            lo = h * RET_QK_DIM
            x1 = acc[:, lo:lo + LANES]
            x2 = acc[:, lo + LANES:lo + 2 * LANES]
            o_ref[:, base + lo:base + lo + LANES] = (x1 * cos - x2 * sin).astype(BF16)
            o_ref[:, base + lo + LANES:base + lo + 2 * LANES] = (x2 * cos + x1 * sin).astype(BF16)

    def epilogue(region, acc, base):
        out = slice(base, base + REGION)
        if region == COL_DA_Q:
            rope_da(acc, base, DA_HEAD_DIM ** -0.5 * LOG2_E)
        elif region == COL_DA_K:
            rope_da(acc, base, 1.0)
        elif region == COL_R_Q:
            rope_ret(acc, base, 1.0)
        elif region == COL_R_K:
            rope_ret(acc, base, RET_QK_DIM ** -0.5)
        elif region == COL_DA_V:
            o_ref[:, out] = acc.astype(BF16)
            if emit_vt:
                vt_ref[0] = acc.T.astype(BF16)
        elif region == COL_R_V:
            o_ref[:, out] = acc.astype(BF16)
        elif region == COL_R_G:
            o_ref[:, out] = (acc * _sigmoid(acc)).astype(BF16)
        else:
            o_ref[:, out] = _sigmoid(acc).astype(BF16)

    for idx, tile in enumerate(tiles):
        @pl.when(t == idx)
        def _(tile=tile):
            for half, (region, w_ref) in enumerate(zip(tile, (wa_ref, wb_ref))):
                base = half * REGION
                if region is None:
                    o_ref[:, base:base + REGION] = jnp.zeros((tm, REGION), BF16)
                else:
                    acc = jnp.dot(xn_ref[...], w_ref[...], preferred_element_type=F32)
                    epilogue(region, acc, base)


def _inproj(x2d, w_bf16, tab, tm, tiles, emit_vt):
    m = x2d.shape[0]
    n_rows = m // tm
    n_tab = tab.shape[0] // tm
    assert len(tiles) >= 2
    first = tuple(a for a, _ in tiles)
    second = tuple(b if b is not None else tiles[idx - 1][1] for idx, (_, b) in enumerate(tiles))
    out_specs = [pl.BlockSpec((tm, 2 * REGION), lambda i, t: (i, t))]
    out_shape = [jax.ShapeDtypeStruct((m, len(tiles) * 2 * REGION), BF16)]
    if emit_vt:
        out_specs.append(pl.BlockSpec((1, REGION, tm), lambda i, t: (i, 0, 0)))
        out_shape.append(jax.ShapeDtypeStruct((n_rows, REGION, tm), BF16))
    return pl.pallas_call(
        functools.partial(_inproj_kernel, tiles=tiles, emit_vt=emit_vt, tm=tm),
        grid=(n_rows, len(tiles)),
        in_specs=[
            pl.BlockSpec(memory_space=pl.ANY),
            pl.BlockSpec((D_MODEL, REGION), lambda i, t: (0, _pick(t, first))),
            pl.BlockSpec((D_MODEL, REGION), lambda i, t: (0, _pick(t, second))),
            pl.BlockSpec((tm, 4 * LANES), lambda i, t: (i % n_tab, 0)),
        ],
        out_specs=out_specs,
        out_shape=out_shape,
        scratch_shapes=[
            pltpu.VMEM((tm, D_MODEL), F32),
            pltpu.VMEM((tm, D_MODEL), BF16),
            pltpu.SemaphoreType.DMA(()),
        ],
        compiler_params=pltpu.CompilerParams(
            dimension_semantics=("arbitrary", "arbitrary"),
            vmem_limit_bytes=VMEM_LIMIT),
        name="inproj",
    )(x2d, w_bf16, w_bf16, tab)


def _attn_kernel(lq1_ref, lk1_ref, lq2_ref, lk2_ref, subg_ref, q_ref, k_ref, vt_ref,
                 km_ref, vmt_ref, *rest, tq, n_q_tiles, n_cast, running_max):
    cast_in = rest[:n_cast]
    o_ref, finite_ref = rest[n_cast:n_cast + 2]
    cast_out = rest[n_cast + 2:2 * n_cast + 2]
    m_scr, l_scr, acc_scr = rest[2 * n_cast + 2:]
    qi = pl.program_id(2)
    every = slice(None)

    for src, dst in zip(cast_in, cast_out):
        dst[...] = src[...].astype(BF16)

    def scores(k_blk, queries):
        out = []
        for c in range(2):
            lanes = slice(c * DA_HEAD_DIM, (c + 1) * DA_HEAD_DIM)
            out.append(lax.dot_general(k_blk[:, lanes], q_ref[queries, lanes], NT_DIMS,
                                       preferred_element_type=F32))
        return out

    def absorb(block_scores, vt_blk, queries, mask, first):
        vt = vt_blk[...]
        vt = jnp.concatenate([vt, jnp.ones((ONES_ROWS, vt.shape[1]), BF16)], axis=0)
        for c, s in enumerate(block_scores):
            if mask is not None:
                s = jnp.where(mask, s, NEG_BIG)
            alpha = None
            if first:
                m_ref = jnp.max(s, axis=0, keepdims=True)
                m_scr[c, :, queries] = m_ref
            elif running_max:
                m_old = m_scr[c, :, queries]
                m_ref = jnp.maximum(m_old, jnp.max(s, axis=0, keepdims=True))
                alpha = jnp.exp2(m_old - m_ref)
                m_scr[c, :, queries] = m_ref
            else:
                m_ref = m_scr[c, :, queries]
            p = jnp.exp2(s - m_ref).astype(BF16)
            pv_ext = jnp.dot(vt, p, preferred_element_type=F32)
            pv = pv_ext[0:DA_V_DIM]
            psum = pv_ext[DA_V_DIM:DA_V_DIM + 1]
            if first:
                l_scr[c, :, queries] = psum
                acc_scr[c, :, queries] = pv
            elif alpha is None:
                l_scr[c, :, queries] += psum
                acc_scr[c, :, queries] += pv
            else:
                l_scr[c, :, queries] = alpha * l_scr[c, :, queries] + psum
                acc_scr[c, :, queries] = alpha * acc_scr[c, :, queries] + pv

    def causal(n_keys, n_q):
        key = lax.broadcasted_iota(jnp.int32, (n_keys, n_q), 0)
        query = lax.broadcasted_iota(jnp.int32, (n_keys, n_q), 1)
        return key <= query

    def key_blocks(n_full):
        meta_mask = lax.broadcasted_iota(jnp.int32, (META_PAD, tq), 0) < N_META
        blocks = [(km_ref, vmt_ref, every, meta_mask, True)]
        for j in range(n_full):
            blocks.append((k_ref.at[j * tq:(j + 1) * tq, :], vt_ref.at[j], every, None, False))
        diag = n_full * tq
        width = tq // DIAG_STEPS
        for d in range(DIAG_STEPS):
            lo = d * width
            blocks.append((k_ref.at[diag + lo:diag + lo + width, :],
                           vt_ref.at[n_full, :, lo:lo + width], slice(lo, tq),
                           causal(width, tq - lo), False))
        return blocks

    for n_full in range(n_q_tiles):
        @pl.when(qi == n_full)
        def _(n_full=n_full):
            blocks = key_blocks(n_full)
            ahead = scores(blocks[0][0], blocks[0][2])
            for t, (_, vt_blk, queries, mask, first) in enumerate(blocks):
                current = ahead
                if t + 1 < len(blocks):
                    ahead = scores(blocks[t + 1][0], blocks[t + 1][2])
                absorb(current, vt_blk, queries, mask, first)

    lam = (jnp.exp(jnp.sum(lq1_ref[...] * lk1_ref[...], axis=-1, keepdims=True))
           - jnp.exp(jnp.sum(lq2_ref[...] * lk2_ref[...], axis=-1, keepdims=True))
           + LAMBDA_INIT)
    l1, l2 = l_scr[0], l_scr[1]
    o_t = acc_scr[0] * (1.0 / l1) - acc_scr[1] * (lam / l2)
    o = o_t.T
    ms = jnp.mean(o * o, axis=-1, keepdims=True)
    o = o * lax.rsqrt(ms + NORM_EPS) * subg_ref[...] * (1.0 - LAMBDA_INIT)
    o_ref[...] = o.astype(BF16)
    big = jnp.finfo(F32).max
    finite_l = jnp.where((l1 <= big) & (l2 <= big), 1.0, 0.0)
    finite_o = jnp.min(jnp.where(jnp.abs(o_t) <= big, 1.0, 0.0), axis=0, keepdims=True)
    finite = jnp.min(jnp.minimum(finite_l, finite_o), axis=1, keepdims=True)
    finite_ref[...] = jnp.broadcast_to(finite, finite_ref.shape)


def _diff_attention(proj, vt, meta_k, meta_vt, lq1, lk1, lq2, lk2, subg, batch, seq, cast,
                    running_max):
    tq = vt.shape[2]
    nq = seq // tq
    n_steps = batch * DA_HEADS * nq
    step = lambda b, h, i: (b * DA_HEADS + h) * nq + i
    vec = pl.BlockSpec((1, DA_HEAD_DIM), lambda b, h, i: (0, 0))
    per_head = REGION // DA_V_DIM
    cast_specs = [pl.BlockSpec((a.shape[0] // n_steps, a.shape[1]),
                               lambda b, h, i: (step(b, h, i), 0)) for a in cast]
    flag_tile = (8, LANES)
    return pl.pallas_call(
        functools.partial(_attn_kernel, tq=tq, n_q_tiles=nq, n_cast=len(cast),
                          running_max=running_max),
        grid=(batch, DA_HEADS, nq),
        in_specs=[
            vec, vec, vec, vec,
            pl.BlockSpec((1, DA_V_DIM), lambda b, h, i: (0, 0)),
            pl.BlockSpec((tq, DA_V_DIM), lambda b, h, i: (b * nq + i, COL_DA_Q * per_head + h)),
            pl.BlockSpec((seq, DA_V_DIM), lambda b, h, i: (b, COL_DA_K * per_head + h)),
            pl.BlockSpec((nq, DA_V_DIM, tq), lambda b, h, i: (b, h, 0)),
            pl.BlockSpec((META_PAD, DA_V_DIM), lambda b, h, i: (0, h)),
            pl.BlockSpec((DA_V_DIM, META_PAD), lambda b, h, i: (h, 0)),
        ] + cast_specs,
        out_specs=[pl.BlockSpec((tq, DA_V_DIM), lambda b, h, i: (b * nq + i, h)),
                   pl.BlockSpec((1,) + flag_tile, lambda b, h, i: (step(b, h, i), 0, 0))]
        + cast_specs,
        out_shape=[jax.ShapeDtypeStruct((batch * seq, DA_HEADS * DA_V_DIM), BF16),
                   jax.ShapeDtypeStruct((n_steps,) + flag_tile, F32)]
        + [jax.ShapeDtypeStruct(a.shape, BF16) for a in cast],
        scratch_shapes=[
            pltpu.VMEM((2, 1, tq), F32),
            pltpu.VMEM((2, 1, tq), F32),
            pltpu.VMEM((2, DA_V_DIM, tq), F32),
        ],
        compiler_params=pltpu.CompilerParams(
            dimension_semantics=("arbitrary", "arbitrary", "arbitrary"),
            vmem_limit_bytes=VMEM_LIMIT),
        name="diff_attention",
    )(lq1, lk1, lq2, lk2, subg, proj, proj, vt, meta_k, meta_vt, *cast)


def _ret_kernel(lg_ref, q_ref, k_ref, v_ref, g_ref, km_ref, vm_ref, o_ref,
                state_scr, decay_scr, xi_scr, zeta_scr, *, chunk):
    b = pl.program_id(0)
    c = pl.program_id(1)
    heads = [(h, lg_ref[h], slice(h * RET_V_DIM, (h + 1) * RET_V_DIM))
             for h in range(RET_HEADS)]

    @pl.when((b == 0) & (c == 0))
    def _():
        row = lax.broadcasted_iota(jnp.int32, (chunk, chunk), 0)
        col = lax.broadcasted_iota(jnp.int32, (chunk, chunk), 1)
        rel = (row - col).astype(F32)
        idx = lax.broadcasted_iota(jnp.int32, (chunk, RET_V_DIM), 0).astype(F32)
        for h, lg, _ in heads:
            decay_scr[h] = jnp.where(rel >= 0, jnp.exp(lg * jnp.maximum(rel, 0.0)), 0.0)
            xi_scr[h] = jnp.exp(lg * (idx + 1.0))
            zeta_scr[h] = jnp.exp(lg * (chunk - 1.0 - idx))

    @pl.when(c == 0)
    def _():
        midx = lax.broadcasted_iota(jnp.int32, (META_PAD, RET_V_DIM), 0).astype(F32)
        for h, lg, cols in heads:
            mz = jnp.exp(lg * (N_META - 1.0 - midx))
            vz = (vm_ref[:, cols].astype(F32) * mz).astype(BF16)
            state_scr[h] = lax.dot_general(km_ref[:, cols], vz, TN_DIMS,
                                           preferred_element_type=F32)

    for h, lg, cols in heads:
        q = q_ref[:, cols]
        k = k_ref[:, cols]
        v = v_ref[:, cols]
        s = lax.dot_general(q, k, NT_DIMS, preferred_element_type=F32) * decay_scr[h]
        inner = jnp.dot(s.astype(BF16), v, preferred_element_type=F32)
        state = state_scr[h]
        cross = jnp.dot(q, state.astype(BF16), preferred_element_type=F32) * xi_scr[h]
        o = inner + cross
        vz = (v.astype(F32) * zeta_scr[h]).astype(BF16)
        chunk_decay = jnp.exp(jnp.full((1, 1), lg * chunk, F32))
        state_scr[h] = chunk_decay * state + lax.dot_general(
            k, vz, TN_DIMS, preferred_element_type=F32)
        ms = jnp.mean(o * o, axis=-1, keepdims=True)
        o_ref[:, cols] = (o * lax.rsqrt(ms + NORM_EPS)
                          * g_ref[:, cols].astype(F32)).astype(BF16)


def _retention(log_gamma, proj, meta_k, meta_v, batch, seq, chunk):
    nc = seq // chunk
    tok = lambda col: pl.BlockSpec((chunk, REGION), lambda b, c, lg: (b * nc + c, col))
    meta = pl.BlockSpec((META_PAD, REGION), lambda b, c, lg: (0, 0))
    grid_spec = pltpu.PrefetchScalarGridSpec(
        num_scalar_prefetch=1,
        grid=(batch, nc),
        in_specs=[tok(COL_R_Q), tok(COL_R_K), tok(COL_R_V), tok(COL_R_G), meta, meta],
        out_specs=pl.BlockSpec((chunk, REGION), lambda b, c, lg: (b * nc + c, 0)),
        scratch_shapes=[
            pltpu.VMEM((RET_HEADS, RET_QK_DIM, RET_V_DIM), F32),
            pltpu.VMEM((RET_HEADS, chunk, chunk), F32),
            pltpu.VMEM((RET_HEADS, chunk, RET_V_DIM), F32),
            pltpu.VMEM((RET_HEADS, chunk, RET_V_DIM), F32),
        ],
    )
    return pl.pallas_call(
        functools.partial(_ret_kernel, chunk=chunk),
        grid_spec=grid_spec,
        out_shape=jax.ShapeDtypeStruct((batch * seq, RET_HEADS * RET_V_DIM), BF16),
        compiler_params=pltpu.CompilerParams(
            dimension_semantics=("arbitrary", "arbitrary"),
            vmem_limit_bytes=VMEM_LIMIT),
        name="retention",
    )(log_gamma, proj, proj, proj, proj, meta_k, meta_v)


def _merge_kernel(oa_ref, or_ref, ga0_ref, ga1_ref, gr0_ref, gr1_ref, x_ref,
                  wpa_ref, wpr_ref, wo_ref, g2_ref, h_ref, hn_ref):
    oa = oa_ref[...]
    orr = or_ref[...]
    h = x_ref[...]
    for n, (ga_ref, gr_ref) in enumerate(((ga0_ref, gr0_ref), (ga1_ref, gr1_ref))):
        cols = slice(n * REGION, (n + 1) * REGION)
        ya = jnp.dot(oa, wpa_ref[:, cols], preferred_element_type=F32)
        yr = jnp.dot(orr, wpr_ref[:, cols], preferred_element_type=F32)
        merged = ga_ref[...].astype(F32) * ya + gr_ref[...].astype(F32) * yr
        h = h + jnp.dot(merged.astype(BF16), wo_ref[cols, :], preferred_element_type=F32)
    h_ref[...] = h
    ms = jnp.mean(h * h, axis=-1, keepdims=True)
    hn_ref[...] = (h * lax.rsqrt(ms + NORM_EPS) * g2_ref[...]).astype(BF16)


def _merge(oa, orr, proj, x2d, wpa, wpr, wo, g2, tm):
    m = x2d.shape[0]
    row = lambda width: pl.BlockSpec((tm, width), lambda i: (i, 0))
    gate = lambda col: pl.BlockSpec((tm, REGION), lambda i: (i, col))
    whole = lambda shape: pl.BlockSpec(shape, lambda i: (0, 0), pipeline_mode=pl.Buffered(1))
    return pl.pallas_call(
        _merge_kernel,
        grid=(m // tm,),
        in_specs=[
            row(REGION), row(REGION),
            gate(COL_G_A), gate(COL_G_A + 1), gate(COL_G_R), gate(COL_G_R + 1),
            row(D_MODEL),
            whole((REGION, D_MODEL)), whole((REGION, D_MODEL)), whole((D_MODEL, D_MODEL)),
            whole((1, D_MODEL)),
        ],
        out_specs=[row(D_MODEL), row(D_MODEL)],
        out_shape=[jax.ShapeDtypeStruct((m, D_MODEL), F32),
                   jax.ShapeDtypeStruct((m, D_MODEL), BF16)],
        compiler_params=pltpu.CompilerParams(
            dimension_semantics=("arbitrary",),
            vmem_limit_bytes=VMEM_LIMIT),
        name="merge",
    )(oa, orr, proj, proj, proj, proj, x2d, wpa, wpr, wo, g2)


def _mlp_kernel(hn_ref, wup_ref, wdown_ref, h_ref, gf_ref, o_ref):
    f = pl.program_id(1)
    last = pl.num_programs(1) - 1

    def ffn_chunk():
        u = jnp.dot(hn_ref[...], wup_ref[...], preferred_element_type=F32)
        a = jnp.square(jnp.maximum(u, 0.0)).astype(BF16)
        return jnp.dot(a, wdown_ref[...], preferred_element_type=F32)

    @pl.when(f == 0)
    def _():
        o_ref[...] = h_ref[...] + ffn_chunk()

    @pl.when((f > 0) & (f < last))
    def _():
        o_ref[...] += ffn_chunk()

    @pl.when(f == last)
    def _():
        y = o_ref[...] + ffn_chunk()
        ms = jnp.mean(y * y, axis=-1, keepdims=True)
        o_ref[...] = y * lax.rsqrt(ms + NORM_EPS) * gf_ref[...]


def _mlp(hn, h1, wup, wdown, gf, tm, tf):
    m = hn.shape[0]
    return pl.pallas_call(
        _mlp_kernel,
        grid=(m // tm, D_FF // tf),
        in_specs=[
            pl.BlockSpec((tm, D_MODEL), lambda i, f: (i, 0)),
            pl.BlockSpec((D_MODEL, tf), lambda i, f: (0, f)),
            pl.BlockSpec((tf, D_MODEL), lambda i, f: (f, 0)),
            pl.BlockSpec((tm, D_MODEL), lambda i, f: (i, 0)),
            pl.BlockSpec((1, D_MODEL), lambda i, f: (0, 0)),
        ],
        out_specs=pl.BlockSpec((tm, D_MODEL), lambda i, f: (i, 0)),
        out_shape=jax.ShapeDtypeStruct((m, D_MODEL), F32),
        compiler_params=pltpu.CompilerParams(
            dimension_semantics=("arbitrary", "arbitrary"),
            vmem_limit_bytes=VMEM_LIMIT),
        name="mlp",
    )(hn, wup, wdown, h1, gf)


def _rope_table(start, count):
    def inv_freq(half):
        return ROPE_THETA ** (-jnp.arange(half, dtype=F32) / half)

    inv_a, inv_r = inv_freq(DA_HEAD_DIM // 2), inv_freq(RET_QK_DIM // 2)
    inv = jnp.concatenate([inv_a, inv_a, inv_r])
    sign = jnp.concatenate([-jnp.ones_like(inv_a), jnp.ones_like(inv_a), jnp.ones_like(inv_r)])

    def cos_sin(pos):
        ang = pos.astype(F32)[:, None] * inv[None, :]
        return jnp.cos(ang), jnp.sin(ang)

    if count <= ROPE_SPLIT or count % ROPE_SPLIT:
        cos, sin = cos_sin(start + jnp.arange(count))
    else:
        cos_c, sin_c = cos_sin(start + ROPE_SPLIT * jnp.arange(count // ROPE_SPLIT))
        cos_f, sin_f = cos_sin(jnp.arange(ROPE_SPLIT))
        cos = (cos_c[:, None] * cos_f[None] - sin_c[:, None] * sin_f[None]).reshape(count, -1)
        sin = (sin_c[:, None] * cos_f[None] + cos_c[:, None] * sin_f[None]).reshape(count, -1)
    return jnp.concatenate([cos, sin * sign[None, :]], axis=-1)


def kernel(x, meta_tokens, norm1_g, w_in, lam_q1, lam_k1, lam_q2, lam_k2, da_subln_g,
           w_pa, w_pr, w_o, norm2_g, w_up, w_down, normf_g):
    batch, seq, d = x.shape
    x2d = x.reshape(batch * seq, d)
    w_in_b = (w_in[0] * norm1_g[0][:, None]).astype(BF16)

    tab_real = _rope_table(N_META, seq)
    tab_meta = _rope_table(0, N_META)

    main_tiles = tuple((r, r + 1 if r + 1 < N_REGIONS else None) for r in range(0, N_REGIONS, 2))
    proj, vt = _inproj(x2d, w_in_b, tab_real, tm=INPROJ_ROWS, tiles=main_tiles, emit_vt=True)
    meta_tiles = ((COL_DA_K, COL_DA_V), (COL_R_K, COL_R_V))
    proj_meta, = _inproj(meta_tokens.astype(F32), w_in_b, tab_meta, tm=N_META,
                         tiles=meta_tiles, emit_vt=False)
    meta_regions = tuple(r for tile in meta_tiles for r in tile)
    pad = ((0, META_PAD - N_META), (0, 0))

    def sl(col):
        at = meta_regions.index(col) * REGION
        return jnp.pad(proj_meta[:, at:at + REGION], pad)

    row = lambda a: a[0].reshape(1, -1)
    attention = functools.partial(
        _diff_attention, proj, vt, sl(COL_DA_K), sl(COL_DA_V).T, row(lam_q1), row(lam_k1),
        row(lam_q2), row(lam_k2), row(da_subln_g), batch, seq)
    oa, finite, w_pa_b, w_pr_b, w_o_b, w_up_b, w_down_b = attention(
        cast=(w_pa[0], w_pr[0], w_o[0], w_up[0], w_down[0]), running_max=False)
    oa = lax.cond(jnp.min(finite) > 0.5, lambda: oa,
                  lambda: attention(cast=(), running_max=True)[0])

    log_gamma = jnp.log(1.0 - 2.0 ** (-5.0 - jnp.arange(RET_HEADS, dtype=F32)))
    orr = _retention(log_gamma, proj, sl(COL_R_K), sl(COL_R_V), batch, seq, chunk=RET_CHUNK)

    h1, hn = _merge(oa, orr, proj, x2d, w_pa_b, w_pr_b, w_o_b, row(norm2_g), tm=MERGE_ROWS)
    out = _mlp(hn, h1, w_up_b, w_down_b, normf_g.reshape(1, d), tm=MLP_ROWS, tf=MLP_FF_TILE)
    return out.reshape(batch, seq, d)
```
